```python
import math
import jax, jax.numpy as jnp
from jax import lax
import numpy as np

D_MODEL = 2048
BATCH = 8
SEQ = 2048
DEPTH = 2

HEAD_DIM = 128
N_HEADS_A = D_MODEL // HEAD_DIM
MOBA_BLOCK = 256
MOBA_TOPK = 3
MOBA_QCHUNK = 32
N_HEADS_B = D_MODEL // (2 * HEAD_DIM)
DIFF_QBLOCK = 128
D_FF = ((8 * D_MODEL // 3 + 255) // 256) * 256
N_EXPERTS = 8
TOP_K = 2
D_FF_EXPERT = 7 * D_MODEL // 2
MOE_BLOCK = 128
N_A_LAYERS = DEPTH // 2
N_B_LAYERS = DEPTH - N_A_LAYERS
N_DENSE = (DEPTH + 1) // 2
N_MOE = DEPTH // 2
RMS_EPS = 1e-6
NEG_INF = -1e30

kernel_name = 'hybrid_moba_diffattn_yoco_moe'


def rmsnorm(x, g):
    x32 = x.astype(jnp.float32)
    y = x32 * lax.rsqrt(jnp.mean(x32 * x32, axis=-1, keepdims=True) + RMS_EPS)
    return y.astype(x.dtype) * g


def alibi_slopes(n_heads):
    h = jnp.arange(1, n_heads + 1, dtype=jnp.float32)
    return jnp.exp2(-8.0 * h / n_heads)


def swiglu(h, w_gate, w_up, w_down):
    return (jax.nn.silu(h @ w_gate) * (h @ w_up)) @ w_down


def moe_swiglu(h, w_router, w_exp_gate, w_exp_up, w_exp_down):
    n_tok, d = h.shape
    logits = (h @ w_router).astype(jnp.float32)
    top_logit, top_idx = lax.top_k(logits, TOP_K)
    gates = jax.nn.softmax(top_logit, axis=-1).astype(h.dtype)
    n_asg = n_tok * TOP_K
    flat_e = top_idx.reshape(-1)
    order = jnp.argsort(flat_e)
    sorted_e = flat_e[order]
    tok = order // TOP_K
    counts = jnp.bincount(flat_e, length=N_EXPERTS)
    padded = ((counts + MOE_BLOCK - 1) // MOE_BLOCK) * MOE_BLOCK
    pend = jnp.cumsum(padded)
    pstart = pend - padded
    ustart = jnp.cumsum(counts) - counts
    dest = pstart[sorted_e] + jnp.arange(n_asg) - ustart[sorted_e]
    n_rows = n_asg + N_EXPERTS * MOE_BLOCK
    n_blocks = n_rows // MOE_BLOCK
    xs = jnp.zeros((n_rows, d), h.dtype).at[dest].set(h[tok])
    block_e = jnp.minimum(jnp.searchsorted(pend, jnp.arange(n_blocks) * MOE_BLOCK, side='right'),
                          N_EXPERTS - 1)

    def expert_block(args):
        xb, e = args
        return (jax.nn.silu(xb @ w_exp_gate[e]) * (xb @ w_exp_up[e])) @ w_exp_down[e]

    ys = lax.map(expert_block, (xs.reshape(n_blocks, MOE_BLOCK, d), block_e)).reshape(n_rows, d)
    contrib = ys[dest] * gates.reshape(-1)[order][:, None]
    return jnp.zeros((n_tok, d), h.dtype).at[tok].add(contrib)


def moba_attention(h, w_qkv, w_o):
    bsz, seq, _ = h.shape
    H, hd, blk = N_HEADS_A, HEAD_DIM, MOBA_BLOCK
    qkv = (h @ w_qkv).reshape(bsz, seq, 3, H, hd)
    q = jnp.moveaxis(qkv[:, :, 0], 2, 1)
    k = jnp.moveaxis(qkv[:, :, 1], 2, 1)
    v = jnp.moveaxis(qkv[:, :, 2], 2, 1)
    n_blk = -(-seq // blk)
    pad = n_blk * blk - seq
    k_blk = jnp.pad(k, ((0, 0), (0, 0), (0, pad), (0, 0))).reshape(bsz, H, n_blk, blk, hd)
    v_blk = jnp.pad(v, ((0, 0), (0, 0), (0, pad), (0, 0))).reshape(bsz, H, n_blk, blk, hd)
    k_mean = jnp.mean(k_blk.astype(jnp.float32), axis=3)
    gate = jnp.einsum('bhsd,bhnd->bhsn', q.astype(jnp.float32), k_mean)
    q_blk_id = jnp.arange(seq) // blk
    fully_past = jnp.arange(n_blk)[None, :] < q_blk_id[:, None]
    gate = jnp.where(fully_past, gate, -jnp.inf)
    k_sel = min(MOBA_TOPK, n_blk)
    _, sel = lax.top_k(gate, k_sel)
    qc = MOBA_QCHUNK
    n_c = seq // qc
    q_steps = q.reshape(bsz, H, n_c, qc, hd).transpose(0, 2, 1, 3, 4).reshape(bsz * n_c, H, qc, hd)
    sel_steps = sel.reshape(bsz, H, n_c, qc, k_sel).transpose(0, 2, 1, 3, 4).reshape(bsz * n_c, H, qc, k_sel)
    b_ids = jnp.repeat(jnp.arange(bsz), n_c)
    c_ids = jnp.tile(jnp.arange(n_c), bsz)
    slopes = alibi_slopes(H)
    scale = hd ** -0.5
    head_ix = jnp.arange(H)[:, None, None]
    key_off = jnp.arange(blk)

    def step(args):
        q_c, sel_c, b, c = args
        kb = k_blk[b]
        vb = v_blk[b]
        t = c * qc + jnp.arange(qc)
        own = (c * qc) // blk
        k_g = kb[head_ix, sel_c]
        v_g = vb[head_ix, sel_c]
        pos_g = sel_c[..., None] * blk + key_off
        dist_g = (t[None, :, None, None] - pos_g).astype(jnp.float32)
        s_g = (jnp.einsum('hqd,hqjkd->hqjk', q_c, k_g).astype(jnp.float32) * scale
               - slopes[:, None, None, None] * dist_g)
        slot_ok = jnp.arange(k_sel) < jnp.minimum(own, k_sel)
        s_g = jnp.where(slot_ok[None, None, :, None], s_g, NEG_INF)
        k_o = lax.dynamic_index_in_dim(kb, own, axis=1, keepdims=False)
        v_o = lax.dynamic_index_in_dim(vb, own, axis=1, keepdims=False)
        pos_o = own * blk + key_off
        dist_o = (t[:, None] - pos_o[None, :]).astype(jnp.float32)
        s_o = (jnp.einsum('hqd,hkd->hqk', q_c, k_o).astype(jnp.float32) * scale
               - slopes[:, None, None] * dist_o)
        s_o = jnp.where((dist_o >= 0)[None], s_o, NEG_INF)
        p = jax.nn.softmax(jnp.concatenate([s_g.reshape(H, qc, k_sel * blk), s_o], axis=-1),
                           axis=-1).astype(q_c.dtype)
        p_g = p[..., :k_sel * blk].reshape(H, qc, k_sel, blk)
        p_o = p[..., k_sel * blk:]
        return (jnp.einsum('hqjk,hqjkd->hqd', p_g, v_g)
                + jnp.einsum('hqk,hkd->hqd', p_o, v_o))

    o = lax.map(step, (q_steps, sel_steps, b_ids, c_ids))
    o = o.reshape(bsz, n_c, H, qc, hd).transpose(0, 1, 3, 2, 4).reshape(bsz, seq, H * hd)
    return o @ w_o


def shared_kv(h, kv_norm_g, w_kv):
    bsz, seq, _ = h.shape
    kv = rmsnorm(h, kv_norm_g) @ w_kv
    k = kv[..., :D_MODEL].reshape(bsz, seq, N_HEADS_B, 2, HEAD_DIM).transpose(0, 2, 3, 1, 4)
    v = kv[..., D_MODEL:].reshape(bsz, seq, N_HEADS_B, 2 * HEAD_DIM).transpose(0, 2, 1, 3)
    return k, v


def diff_attention(h, k, v, w_q, lam_q1, lam_k1, lam_q2, lam_k2, subln_g, w_o, lam_init):
    bsz, seq, _ = h.shape
    H, hd, qb = N_HEADS_B, HEAD_DIM, DIFF_QBLOCK
    n_qb = seq // qb
    q = (h @ w_q).reshape(bsz, n_qb, qb, H, 2, hd).transpose(1, 0, 3, 4, 2, 5)
    lam = (jnp.exp(jnp.sum(lam_q1.astype(jnp.float32) * lam_k1.astype(jnp.float32)))
           - jnp.exp(jnp.sum(lam_q2.astype(jnp.float32) * lam_k2.astype(jnp.float32)))
           + lam_init)
    slopes = alibi_slopes(H)
    scale = hd ** -0.5
    key_pos = jnp.arange(seq)

    def step(args):
        q_b, i = args
        t = i * qb + jnp.arange(qb)
        dist = (t[:, None] - key_pos[None, :]).astype(jnp.float32)
        s = (jnp.einsum('bhcqd,bhckd->bhcqk', q_b, k).astype(jnp.float32) * scale
             - slopes[:, None, None, None] * dist)
        s = jnp.where(dist >= 0, s, NEG_INF)
        a = jax.nn.softmax(s, axis=-1)
        w = (a[:, :, 0] - lam * a[:, :, 1]).astype(v.dtype)
        return jnp.einsum('bhqk,bhkd->bhqd', w, v)

    o = lax.map(step, (q, jnp.arange(n_qb)))
    o = rmsnorm(o, subln_g) * (1.0 - lam_init)
    o = o.transpose(1, 0, 3, 2, 4).reshape(bsz, seq, H * 2 * hd)
    return o @ w_o


def setup_inputs(seed: int = 0) -> dict:
    key = jax.random.key(seed)
    ks = jax.random.split(key, 32)
    f32 = jnp.float32

    def nrm(k, shape, fan_in):
        return jax.random.normal(k, shape, f32) * fan_in ** -0.5

    def gain(k, shape):
        return 1.0 + 0.02 * jax.random.normal(k, shape, f32)

    D, F, E, FE, hd = D_MODEL, D_FF, N_EXPERTS, D_FF_EXPERT, HEAD_DIM
    return {
        'x': jax.random.normal(ks[0], (BATCH, SEQ, D), f32),
        'attn_a_norm_g': gain(ks[1], (N_A_LAYERS, D)),
        'w_qkv_a': nrm(ks[2], (N_A_LAYERS, D, 3 * D), D),
        'w_o_a': nrm(ks[3], (N_A_LAYERS, D, D), D),
        'kv_norm_g': gain(ks[4], (D,)),
        'w_kv': nrm(ks[5], (D, 2 * D), D),
        'attn_b_norm_g': gain(ks[6], (N_B_LAYERS, D)),
        'w_q_b': nrm(ks[7], (N_B_LAYERS, D, D), D),
        'lam_q1': 0.1 * jax.random.normal(ks[8], (N_B_LAYERS, hd), f32),
        'lam_k1': 0.1 * jax.random.normal(ks[9], (N_B_LAYERS, hd), f32),
        'lam_q2': 0.1 * jax.random.normal(ks[10], (N_B_LAYERS, hd), f32),
        'lam_k2': 0.1 * jax.random.normal(ks[11], (N_B_LAYERS, hd), f32),
        'subln_g': gain(ks[12], (N_B_LAYERS, 2 * hd)),
        'w_o_b': nrm(ks[13], (N_B_LAYERS, D, D), D),
        'ffn_norm_g': gain(ks[14], (DEPTH, D)),
        'w_gate': nrm(ks[15], (N_DENSE, D, F), D),
        'w_up': nrm(ks[16], (N_DENSE, D, F), D),
        'w_down': nrm(ks[17], (N_DENSE, F, D), F),
        'w_router': nrm(ks[18], (N_MOE, D, E), D),
        'w_exp_gate': nrm(ks[19], (N_MOE, E, D, FE), D),
        'w_exp_up': nrm(ks[20], (N_MOE, E, D, FE), D),
        'w_exp_down': nrm(ks[21], (N_MOE, E, FE, D), FE),
        'final_norm_g': gain(ks[22], (D,)),
    }


def reference(x, attn_a_norm_g, w_qkv_a, w_o_a, kv_norm_g, w_kv, attn_b_norm_g, w_q_b,
              lam_q1, lam_k1, lam_q2, lam_k2, subln_g, w_o_b, ffn_norm_g, w_gate, w_up,
              w_down, w_router, w_exp_gate, w_exp_up, w_exp_down, final_norm_g):
    bsz, seq, d = x.shape
    kv_k, kv_v = None, None
    for layer in range(DEPTH):
        if layer < N_A_LAYERS:
            i = layer
            x = x + moba_attention(rmsnorm(x, attn_a_norm_g[i]), w_qkv_a[i], w_o_a[i])
        else:
            i = layer - N_A_LAYERS
            if i == 0:
                kv_k, kv_v = shared_kv(x, kv_norm_g, w_kv)
            lam_init = 0.8 - 0.6 * math.exp(-0.3 * layer)
            x = x + diff_attention(rmsnorm(x, attn_b_norm_g[i]), kv_k, kv_v, w_q_b[i],
                                   lam_q1[i], lam_k1[i], lam_q2[i], lam_k2[i],
                                   subln_g[i], w_o_b[i], lam_init)
        hn = rmsnorm(x, ffn_norm_g[layer])
        j = layer // 2
        if layer % 2 == 0:
            x = x + swiglu(hn, w_gate[j], w_up[j], w_down[j])
        else:
            x = x + moe_swiglu(hn.reshape(bsz * seq, d), w_router[j], w_exp_gate[j],
                               w_exp_up[j], w_exp_down[j]).reshape(bsz, seq, d)
    return rmsnorm(x, final_norm_g)
```

```python
import functools
import math

import jax
import jax.numpy as jnp
from jax import lax
from jax.experimental import pallas as pl
from jax.experimental.pallas import tpu as pltpu

F32 = jnp.float32
BF16 = jnp.bfloat16

HEAD_DIM = 128
MOBA_BLOCK = 256
MOBA_TOPK = 3
N_EXPERTS = 8
TOP_K = 2
RMS_EPS = 1e-6
NEG_INF = -1e30

VMEM_LIMIT_BYTES = 56 * 1024 * 1024

ROW_TILE = 1024
COL_TILE = 1024
FFN_COL_TILE = 512
ATT_BLOCK = 256
MOE_ROW_TILE = 512
MOE_F_TILE = 512
ROUTER_TILE = 512
COMBINE_TILE = 256

_NT = (((1,), (1,)), ((), ()))


def _params(*sem):
    return pltpu.CompilerParams(dimension_semantics=sem, vmem_limit_bytes=VMEM_LIMIT_BYTES)


def _rms_normalize(x, g):
    ms = jnp.mean(x * x, axis=-1, keepdims=True)
    return (x * lax.rsqrt(ms + RMS_EPS)) * g


def _norm_matmul_kernel(x_ref, g_ref, w_ref, o_ref, xn_ref):
    @pl.when(pl.program_id(1) == 0)
    def _():
        xn_ref[...] = _rms_normalize(x_ref[...], g_ref[...]).astype(BF16)

    o_ref[...] = jnp.dot(xn_ref[...], w_ref[...], preferred_element_type=F32).astype(o_ref.dtype)


def norm_matmul(x, g, w, out_dtype=BF16):
    n, d = x.shape
    cols = w.shape[1]
    tm, tn = ROW_TILE, min(COL_TILE, cols)
    return pl.pallas_call(
        _norm_matmul_kernel,
        out_shape=jax.ShapeDtypeStruct((n, cols), out_dtype),
        grid=(n // tm, cols // tn),
        in_specs=[
            pl.BlockSpec((tm, d), lambda i, j: (i, 0)),
            pl.BlockSpec((1, d), lambda i, j: (0, 0)),
            pl.BlockSpec((d, tn), lambda i, j: (0, j)),
        ],
        out_specs=pl.BlockSpec((tm, tn), lambda i, j: (i, j)),
        scratch_shapes=[pltpu.VMEM((tm, d), BF16)],
        compiler_params=_params("parallel", "arbitrary"),
        name="norm_matmul",
    )(x, g.reshape(1, d), w)


def _silu(g):
    return g * (1.0 / (1.0 + jnp.exp(-g)))


def _norm_swiglu_up_kernel(x_ref, g_ref, wg_ref, wu_ref, o_ref, xn_ref):
    @pl.when(pl.program_id(1) == 0)
    def _():
        xn_ref[...] = _rms_normalize(x_ref[...], g_ref[...]).astype(BF16)

    xn = xn_ref[...]
    gate = jnp.dot(xn, wg_ref[...], preferred_element_type=F32)
    up = jnp.dot(xn, wu_ref[...], preferred_element_type=F32)
    o_ref[...] = (_silu(gate) * up).astype(o_ref.dtype)


def norm_swiglu_up(x, g, wg, wu):
    n, d = x.shape
    f = wg.shape[1]
    tm, tf = ROW_TILE, FFN_COL_TILE
    return pl.pallas_call(
        _norm_swiglu_up_kernel,
        out_shape=jax.ShapeDtypeStruct((n, f), BF16),
        grid=(n // tm, f // tf),
        in_specs=[
            pl.BlockSpec((tm, d), lambda i, j: (i, 0)),
            pl.BlockSpec((1, d), lambda i, j: (0, 0)),
            pl.BlockSpec((d, tf), lambda i, j: (0, j)),
            pl.BlockSpec((d, tf), lambda i, j: (0, j)),
        ],
        out_specs=pl.BlockSpec((tm, tf), lambda i, j: (i, j)),
        scratch_shapes=[pltpu.VMEM((tm, d), BF16)],
        compiler_params=_params("parallel", "arbitrary"),
        name="norm_swiglu_up",
    )(x, g.reshape(1, d), wg, wu)


def _matmul_residual_kernel(a_ref, w_ref, r_ref, o_ref):
    o_ref[...] = r_ref[...] + jnp.dot(a_ref[...], w_ref[...], preferred_element_type=F32)


def matmul_residual(a, w, res):
    n, k = a.shape
    cols = w.shape[1]
    tm, tn = ROW_TILE, min(COL_TILE, cols)
    while 2 * (2 * k * (tm + tn) + 8 * tm * tn) + 4 * tm * tn > VMEM_LIMIT_BYTES:
        tn //= 2
    return pl.pallas_call(
        _matmul_residual_kernel,
        out_shape=jax.ShapeDtypeStruct((n, cols), F32),
        grid=(n // tm, cols // tn),
        in_specs=[
            pl.BlockSpec((tm, k), lambda i, j: (i, 0)),
            pl.BlockSpec((k, tn), lambda i, j: (0, j)),
            pl.BlockSpec((tm, tn), lambda i, j: (i, j)),
        ],
        out_specs=pl.BlockSpec((tm, tn), lambda i, j: (i, j)),
        compiler_params=_params("parallel", "arbitrary"),
        name="matmul_residual",
    )(a, w, res)


def _moba_kernel(slopes_ref, q_ref, k_ref, v_ref, o_ref, kmean_ref, m_ref, l_ref, acc_ref, *, n_blk):
    blk = ATT_BLOCK
    h = pl.program_id(1)
    i = pl.program_id(2)
    slope = slopes_ref[h]
    scale = HEAD_DIM ** -0.5

    @pl.when(i == 0)
    def _():
        for j in range(n_blk):
            kj = k_ref[j * blk:(j + 1) * blk, :].astype(F32)
            kmean_ref[j:j + 1, :] = jnp.mean(kj, axis=0, keepdims=True)

    q = q_ref[...]

    km = kmean_ref[...]
    km_hi = km.astype(BF16)
    km_lo = (km - km_hi.astype(F32)).astype(BF16)
    gate_t = (lax.dot_general(km_hi, q, _NT, preferred_element_type=F32)
              + lax.dot_general(km_lo, q, _NT, preferred_element_type=F32))

    blk_id = lax.broadcasted_iota(jnp.int32, gate_t.shape, 0)
    cnt = jnp.zeros(gate_t.shape, F32)
    for jp in range(n_blk):
        row = gate_t[jp:jp + 1, :]
        beats = (row > gate_t) | ((row == gate_t) & (jp < blk_id))
        cnt = cnt + jnp.where(beats, (jp < i).astype(F32), 0.0)
    sel_t = jnp.where((cnt < float(MOBA_TOPK)) & (blk_id < i), 1.0, 0.0).astype(BF16)
    eye = (lax.broadcasted_iota(jnp.int32, (blk, blk), 0)
           == lax.broadcasted_iota(jnp.int32, (blk, blk), 1)).astype(BF16)
    sel = lax.dot_general(eye, sel_t, _NT, preferred_element_type=F32)

    rel0 = (lax.broadcasted_iota(jnp.int32, (blk, blk), 0)
            - lax.broadcasted_iota(jnp.int32, (blk, blk), 1))

    def scores(j, dist_blocks):
        kj = k_ref[j * blk:(j + 1) * blk, :]
        s = lax.dot_general(q, kj, _NT, preferred_element_type=F32) * scale
        dist = (rel0 + dist_blocks * blk).astype(F32)
        return s - slope * dist

    for j in range(n_blk):
        @pl.when(i == j)
        def _(j=j):
            s = jnp.where(rel0 >= 0, scores(j, 0), NEG_INF)
            m = jnp.max(s, axis=1, keepdims=True)
            p = jnp.exp(s - m)
            m_ref[...] = m
            l_ref[...] = jnp.sum(p, axis=1, keepdims=True)
            acc_ref[...] = jnp.dot(p.astype(BF16), v_ref[j * blk:(j + 1) * blk, :],
                                   preferred_element_type=F32)

    for j in range(n_blk - 1):
        @pl.when(j < i)
        def _(j=j):
            s = jnp.where(sel[:, j:j + 1] > 0.5, scores(j, i - j), NEG_INF)
            m_prev = m_ref[...]
            m_new = jnp.maximum(m_prev, jnp.max(s, axis=1, keepdims=True))
            alpha = jnp.exp(m_prev - m_new)
            p = jnp.exp(s - m_new)
            m_ref[...] = m_new
            l_ref[...] = alpha * l_ref[...] + jnp.sum(p, axis=1, keepdims=True)
            acc_ref[...] = alpha * acc_ref[...] + jnp.dot(
                p.astype(BF16), v_ref[j * blk:(j + 1) * blk, :], preferred_element_type=F32)

    o_ref[...] = (acc_ref[...] / l_ref[...]).astype(o_ref.dtype)


def moba_attention(qkv, slopes, bsz, seq, n_heads):
    n = bsz * seq
    blk, hd = ATT_BLOCK, HEAD_DIM
    n_blk = seq // blk
    return pl.pallas_call(
        functools.partial(_moba_kernel, n_blk=n_blk),
        out_shape=jax.ShapeDtypeStruct((n, n_heads * hd), BF16),
        grid=(bsz, n_heads, n_blk),
        in_specs=[
            pl.BlockSpec(memory_space=pltpu.SMEM),
            pl.BlockSpec((blk, hd), lambda b, h, i: (b * n_blk + i, h)),
            pl.BlockSpec((seq, hd), lambda b, h, i: (b, n_heads + h)),
            pl.BlockSpec((seq, hd), lambda b, h, i: (b, 2 * n_heads + h)),
        ],
        out_specs=pl.BlockSpec((blk, hd), lambda b, h, i: (b * n_blk + i, h)),
        scratch_shapes=[
            pltpu.VMEM((n_blk, hd), F32),
            pltpu.VMEM((blk, 1), F32),
            pltpu.VMEM((blk, 1), F32),
            pltpu.VMEM((blk, hd), F32),
        ],
        compiler_params=_params("parallel", "parallel", "arbitrary"),
        name="moba_attention",
    )(slopes, qkv, qkv, qkv)


def _diff_kernel(slopes_ref, q_ref, k_ref, v_ref, lq1_ref, lk1_ref, lq2_ref, lk2_ref, sg_ref,
                 o_ref, m_ref, l_ref, acc_ref, *, n_blk, lam_init):
    blk, hd = ATT_BLOCK, HEAD_DIM
    h = pl.program_id(1)
    i = pl.program_id(2)
    slope = slopes_ref[h]
    scale = hd ** -0.5

    rel0 = (lax.broadcasted_iota(jnp.int32, (blk, blk), 0)
            - lax.broadcasted_iota(jnp.int32, (blk, blk), 1))

    m_ref[...] = jnp.full(m_ref.shape, -jnp.inf, F32)
    l_ref[...] = jnp.zeros(l_ref.shape, F32)
    acc_ref[...] = jnp.zeros(acc_ref.shape, F32)

    for j in range(n_blk):
        @pl.when(j <= i)
        def _(j=j):
            rel = rel0 + (i - j) * blk
            bias = slope * rel.astype(F32)
            vj = v_ref[j * blk:(j + 1) * blk, :]
            for c in range(2):
                qc = q_ref[:, c * hd:(c + 1) * hd]
                kc = k_ref[j * blk:(j + 1) * blk, c * hd:(c + 1) * hd]
                s = lax.dot_general(qc, kc, _NT, preferred_element_type=F32) * scale - bias
                s = jnp.where(rel >= 0, s, NEG_INF)
                m_prev = m_ref[c]
                m_new = jnp.maximum(m_prev, jnp.max(s, axis=1, keepdims=True))
                alpha = jnp.exp(m_prev - m_new)
                p = jnp.exp(s - m_new)
                m_ref[c] = m_new
                l_ref[c] = alpha * l_ref[c] + jnp.sum(p, axis=1, keepdims=True)
                acc_ref[c] = alpha * acc_ref[c] + jnp.dot(p.astype(BF16), vj,
                                                          preferred_element_type=F32)

    lam = (jnp.exp(jnp.sum(lq1_ref[...] * lk1_ref[...], axis=1, keepdims=True))
           - jnp.exp(jnp.sum(lq2_ref[...] * lk2_ref[...], axis=1, keepdims=True))
           + lam_init)
    o = acc_ref[0] / l_ref[0] - lam * (acc_ref[1] / l_ref[1])
    o = _rms_normalize(o, sg_ref[...]) * (1.0 - lam_init)
    o_ref[...] = o.astype(o_ref.dtype)


def diff_attention(q, kv, slopes, lq1, lk1, lq2, lk2, subln_g, lam_init, bsz, seq, n_heads):
    n = bsz * seq
    blk, hd = ATT_BLOCK, HEAD_DIM
    n_blk = seq // blk
    vec = lambda a: a.reshape(1, hd)
    vec_spec = pl.BlockSpec((1, hd), lambda b, h, i: (0, 0))
    return pl.pallas_call(
        functools.partial(_diff_kernel, n_blk=n_blk, lam_init=lam_init),
        out_shape=jax.ShapeDtypeStruct((n, n_heads * 2 * hd), BF16),
        grid=(bsz, n_heads, n_blk),
        in_specs=[
            pl.BlockSpec(memory_space=pltpu.SMEM),
            pl.BlockSpec((blk, 2 * hd), lambda b, h, i: (b * n_blk + i, h)),
            pl.BlockSpec((seq, 2 * hd), lambda b, h, i: (b, h)),
            pl.BlockSpec((seq, 2 * hd), lambda b, h, i: (b, n_heads + h)),
            vec_spec, vec_spec, vec_spec, vec_spec,
            pl.BlockSpec((1, 2 * hd), lambda b, h, i: (0, 0)),
        ],
        out_specs=pl.BlockSpec((blk, 2 * hd), lambda b, h, i: (b * n_blk + i, h)),
        scratch_shapes=[
            pltpu.VMEM((2, blk, 1), F32),
            pltpu.VMEM((2, blk, 1), F32),
            pltpu.VMEM((2, blk, 2 * hd), F32),
        ],
        compiler_params=_params("parallel", "parallel", "arbitrary"),
        name="diff_attention",
    )(slopes, q, kv, kv, vec(lq1), vec(lk1), vec(lq2), vec(lk2), subln_g.reshape(1, 2 * hd))


def _router_kernel(x_ref, g_ref, w_ref, o_ref):
    hn = _rms_normalize(x_ref[...], g_ref[...])
    w = w_ref[...]
    h_hi = hn.astype(BF16)
    h_lo = (hn - h_hi.astype(F32)).astype(BF16)
    w_hi = w.astype(BF16)
    w_lo = (w - w_hi.astype(F32)).astype(BF16)
    logits = (jnp.dot(h_hi, w_hi, preferred_element_type=F32)
              + jnp.dot(h_hi, w_lo, preferred_element_type=F32)
              + jnp.dot(h_lo, w_hi, preferred_element_type=F32))
    lane = lax.broadcasted_iota(jnp.int32, logits.shape, 1).astype(F32)
    big = float(N_EXPERTS)
    m1 = jnp.max(logits, axis=1, keepdims=True)
    e1 = jnp.min(jnp.where(logits == m1, lane, big), axis=1, keepdims=True)
    rest = jnp.where(lane == e1, -jnp.inf, logits)
    m2 = jnp.max(rest, axis=1, keepdims=True)
    e2 = jnp.min(jnp.where(rest == m2, lane, big), axis=1, keepdims=True)
    t = jnp.exp(m2 - m1)
    g1 = 1.0 / (1.0 + t)
    g2 = t / (1.0 + t)
    o_ref[...] = jnp.where(lane == 0.0, e1,
                 jnp.where(lane == 1.0, e2,
                 jnp.where(lane == 2.0, g1,
                 jnp.where(lane == 3.0, g2, 0.0))))


def router(x, g, w_router):
    n, d = x.shape
    tm = ROUTER_TILE
    return pl.pallas_call(
        _router_kernel,
        out_shape=jax.ShapeDtypeStruct((n, N_EXPERTS), F32),
        grid=(n // tm,),
        in_specs=[
            pl.BlockSpec((tm, d), lambda i: (i, 0)),
            pl.BlockSpec((1, d), lambda i: (0, 0)),
            pl.BlockSpec((d, N_EXPERTS), lambda i: (0, 0)),
        ],
        out_specs=pl.BlockSpec((tm, N_EXPERTS), lambda i: (i, 0)),
        compiler_params=_params("parallel"),
        name="router",
    )(x, g.reshape(1, d), w_router)


def _dispatch_kernel(row_tok_ref, nvalid_ref, x_hbm, g_ref, o_ref, buf_ref, sem):
    t = pl.program_id(0)
    tm = buf_ref.shape[0]

    @pl.when(t < nvalid_ref[0])
    def _():
        def issue(r, carry):
            tok = row_tok_ref[t * tm + r]
            pltpu.make_async_copy(x_hbm.at[pl.ds(tok, 1), :], buf_ref.at[pl.ds(r, 1), :], sem).start()
            return carry
        lax.fori_loop(0, tm, issue, 0)
        pltpu.make_async_copy(x_hbm.at[pl.ds(0, tm), :], buf_ref, sem).wait()
        o_ref[...] = _rms_normalize(buf_ref[...], g_ref[...]).astype(o_ref.dtype)

    @pl.when(t >= nvalid_ref[0])
    def _():
        o_ref[...] = jnp.zeros(o_ref.shape, o_ref.dtype)


def dispatch(x, g, row_tok, n_valid_tiles, n_rows):
    n, d = x.shape
    tm = MOE_ROW_TILE
    return pl.pallas_call(
        _dispatch_kernel,
        out_shape=jax.ShapeDtypeStruct((n_rows, d), BF16),
        grid_spec=pltpu.PrefetchScalarGridSpec(
            num_scalar_prefetch=2,
            grid=(n_rows // tm,),
            in_specs=[
                pl.BlockSpec(memory_space=pl.ANY),
                pl.BlockSpec((1, d), lambda t, rt, nv: (0, 0)),
            ],
            out_specs=pl.BlockSpec((tm, d), lambda t, rt, nv: (t, 0)),
            scratch_shapes=[pltpu.VMEM((tm, d), F32), pltpu.SemaphoreType.DMA],
        ),
        compiler_params=_params("arbitrary"),
        name="moe_dispatch",
    )(row_tok, n_valid_tiles, x, g.reshape(1, d))


def _expert_kernel(tile_e_ref, nvalid_ref, xs_ref, wg_ref, wu_ref, wd_ref, o_ref):
    t = pl.program_id(0)
    f = pl.program_id(1)

    @pl.when(f == 0)
    def _():
        o_ref[...] = jnp.zeros(o_ref.shape, o_ref.dtype)

    @pl.when(t < nvalid_ref[0])
    def _():
        xs = xs_ref[...]
        gate = jnp.dot(xs, wg_ref[...], preferred_element_type=F32)
        up = jnp.dot(xs, wu_ref[...], preferred_element_type=F32)
        act = (_silu(gate) * up).astype(BF16)
        o_ref[...] += jnp.dot(act, wd_ref[...], preferred_element_type=F32)


def expert_ffn(xs, wg, wu, wd, tile_e, n_valid_tiles):
    n_rows, d = xs.shape
    fe = wg.shape[2]
    tm, tf = MOE_ROW_TILE, MOE_F_TILE
    nf = fe // tf

    def f_idx(t, f, nv):
        return jnp.where(t < nv[0], f, nf - 1)

    return pl.pallas_call(
        _expert_kernel,
        out_shape=jax.ShapeDtypeStruct((n_rows, d), F32),
        grid_spec=pltpu.PrefetchScalarGridSpec(
            num_scalar_prefetch=2,
            grid=(n_rows // tm, nf),
            in_specs=[
                pl.BlockSpec((tm, d), lambda t, f, te, nv: (t, 0)),
                pl.BlockSpec((None, d, tf), lambda t, f, te, nv: (te[t], 0, f_idx(t, f, nv))),
                pl.BlockSpec((None, d, tf), lambda t, f, te, nv: (te[t], 0, f_idx(t, f, nv))),
                pl.BlockSpec((None, tf, d), lambda t, f, te, nv: (te[t], f_idx(t, f, nv), 0)),
            ],
            out_specs=pl.BlockSpec((tm, d), lambda t, f, te, nv: (t, 0)),
        ),
        compiler_params=_params("arbitrary", "arbitrary"),
        name="moe_experts",
    )(tile_e, n_valid_tiles, xs, wg, wu, wd)


def _combine_kernel(dest_ref, x_ref, gates_ref, g_ref, ys_hbm, o_ref, buf_ref, sem):
    t = pl.program_id(0)
    tm = x_ref.shape[0]

    def issue(r, carry):
        for k in range(TOP_K):
            row = dest_ref[(t * tm + r) * TOP_K + k]
            pltpu.make_async_copy(ys_hbm.at[pl.ds(row, 1), :], buf_ref.at[k, pl.ds(r, 1), :], sem).start()
        return carry
    lax.fori_loop(0, tm, issue, 0)
    pltpu.make_async_copy(buf_ref, buf_ref, sem).wait()
    gates = gates_ref[...]
    y = x_ref[...] + gates[:, 2:3] * buf_ref[0] + gates[:, 3:4] * buf_ref[1]
    o_ref[...] = _rms_normalize(y, g_ref[...])


def combine(x, gates, final_g, ys, dest):
    n, d = x.shape
    tm = COMBINE_TILE
    return pl.pallas_call(
        _combine_kernel,
        out_shape=jax.ShapeDtypeStruct((n, d), F32),
        grid_spec=pltpu.PrefetchScalarGridSpec(
            num_scalar_prefetch=1,
            grid=(n // tm,),
            in_specs=[
                pl.BlockSpec((tm, d), lambda t, dst: (t, 0)),
                pl.BlockSpec((tm, N_EXPERTS), lambda t, dst: (t, 0)),
                pl.BlockSpec((1, d), lambda t, dst: (0, 0)),
                pl.BlockSpec(memory_space=pl.ANY),
            ],
            out_specs=pl.BlockSpec((tm, d), lambda t, dst: (t, 0)),
            scratch_shapes=[pltpu.VMEM((TOP_K, tm, d), F32), pltpu.SemaphoreType.DMA],
        ),
        compiler_params=_params("arbitrary"),
        name="moe_combine",
    )(dest, x, gates, final_g.reshape(1, d), ys)


def _routing_tables(gates_out, n_tok):
    tm = MOE_ROW_TILE
    n_asg = n_tok * TOP_K
    n_rows = n_asg + N_EXPERTS * tm
    n_tiles = n_rows // tm
    flat_e = gates_out[:, :TOP_K].astype(jnp.int32).reshape(-1)
    onehot = (flat_e[:, None] == jnp.arange(N_EXPERTS, dtype=jnp.int32)[None, :]).astype(jnp.int32)
    csum = jnp.cumsum(onehot, axis=0)
    rank = jnp.sum(csum * onehot, axis=1) - 1
    counts = csum[-1]
    padded = ((counts + tm - 1) // tm) * tm
    pend = jnp.cumsum(padded)
    pstart = pend - padded
    dest = (jnp.sum(pstart[None, :] * onehot, axis=1) + rank).astype(jnp.int32)
    row_tok = jnp.zeros((n_rows,), jnp.int32).at[dest].set(
        jnp.arange(n_asg, dtype=jnp.int32) // TOP_K)
    n_valid = (pend[-1] // tm).astype(jnp.int32)
    tile_start = jnp.minimum(jnp.arange(n_tiles, dtype=jnp.int32), n_valid - 1) * tm
    tile_e = jnp.minimum(jnp.sum((pend[None, :] <= tile_start[:, None]).astype(jnp.int32), axis=1),
                         N_EXPERTS - 1).astype(jnp.int32)
    return dest, row_tok, tile_e, n_valid.reshape(1), n_rows


def kernel(x, attn_a_norm_g, w_qkv_a, w_o_a, kv_norm_g, w_kv, attn_b_norm_g, w_q_b,
           lam_q1, lam_k1, lam_q2, lam_k2, subln_g, w_o_b, ffn_norm_g, w_gate, w_up,
           w_down, w_router, w_exp_gate, w_exp_up, w_exp_down, final_norm_g):
    bsz, seq, d = x.shape
    n = bsz * seq
    n_heads_a = d // HEAD_DIM
    n_heads_b = d // (2 * HEAD_DIM)
    assert ffn_norm_g.shape[0] == 2 and attn_a_norm_g.shape[0] == 1 and attn_b_norm_g.shape[0] == 1
    bf = lambda w: w.astype(BF16)

    def slopes(n_heads):
        hh = jnp.arange(1, n_heads + 1, dtype=F32)
        return jnp.exp2(-8.0 * hh / n_heads)

    xf = x.reshape(n, d)

    qkv = norm_matmul(xf, attn_a_norm_g[0], bf(w_qkv_a[0]))
    o = moba_attention(qkv, slopes(n_heads_a), bsz, seq, n_heads_a)
    xf = matmul_residual(o, bf(w_o_a[0]), xf)
    act = norm_swiglu_up(xf, ffn_norm_g[0], bf(w_gate[0]), bf(w_up[0]))
    xf = matmul_residual(act, bf(w_down[0]), xf)

    kv = norm_matmul(xf, kv_norm_g, bf(w_kv))
    q = norm_matmul(xf, attn_b_norm_g[0], bf(w_q_b[0]))
    lam_init = 0.8 - 0.6 * math.exp(-0.3 * 1)
    o = diff_attention(q, kv, slopes(n_heads_b), lam_q1[0], lam_k1[0], lam_q2[0], lam_k2[0],
                       subln_g[0], lam_init, bsz, seq, n_heads_b)
    xf = matmul_residual(o, bf(w_o_b[0]), xf)
    gates_out = router(xf, ffn_norm_g[1], w_router[0])
    dest, row_tok, tile_e, n_valid, n_rows = _routing_tables(gates_out, n)
    xs = dispatch(xf, ffn_norm_g[1], row_tok, n_valid, n_rows)
    ys = expert_ffn(xs, bf(w_exp_gate[0]), bf(w_exp_up[0]), bf(w_exp_down[0]), tile_e, n_valid)
    out = combine(xf, gates_out, final_norm_g, ys, dest)
    return out.reshape(bsz, seq, d)
```

```python
import functools
import math

import jax
import jax.numpy as jnp
from jax import lax
from jax.experimental import pallas as pl
from jax.experimental.pallas import tpu as pltpu

F32 = jnp.float32
BF16 = jnp.bfloat16

HEAD_DIM = 128
MOBA_BLOCK = 256
MOBA_TOPK = 3
N_EXPERTS = 8
TOP_K = 2
RMS_EPS = 1e-6
NEG_INF = -1e30

VMEM_LIMIT_BYTES = 56 * 1024 * 1024

ROW_TILE = 1024
COL_TILE = 1024
FFN_COL_TILE = 512
ATT_BLOCK = 256
MOE_ROW_TILE = 512
MOE_F_TILE = 512
ROUTER_TILE = 512
COMBINE_TILE = 256

_NT = (((1,), (1,)), ((), ()))


def _params(*sem):
    return pltpu.CompilerParams(dimension_semantics=sem, vmem_limit_bytes=VMEM_LIMIT_BYTES)


def _rms_normalize(x, g):
    ms = jnp.mean(x * x, axis=-1, keepdims=True)
    return (x * lax.rsqrt(ms + RMS_EPS)) * g


def _norm_matmul_kernel(x_ref, g_ref, w_ref, o_ref, xn_ref):
    @pl.when(pl.program_id(1) == 0)
    def _():
        xn_ref[...] = _rms_normalize(x_ref[...], g_ref[...]).astype(BF16)

    o_ref[...] = jnp.dot(xn_ref[...], w_ref[...], preferred_element_type=F32).astype(o_ref.dtype)


def norm_matmul(x, g, w, out_dtype=BF16):
    n, d = x.shape
    cols = w.shape[1]
    tm, tn = ROW_TILE, min(COL_TILE, cols)
    return pl.pallas_call(
        _norm_matmul_kernel,
        out_shape=jax.ShapeDtypeStruct((n, cols), out_dtype),
        grid=(n // tm, cols // tn),
        in_specs=[
            pl.BlockSpec((tm, d), lambda i, j: (i, 0)),
            pl.BlockSpec((1, d), lambda i, j: (0, 0)),
            pl.BlockSpec((d, tn), lambda i, j: (0, j)),
        ],
        out_specs=pl.BlockSpec((tm, tn), lambda i, j: (i, j)),
        scratch_shapes=[pltpu.VMEM((tm, d), BF16)],
        compiler_params=_params("parallel", "arbitrary"),
        name="norm_matmul",
    )(x, g.reshape(1, d), w)


def _silu(g):
    return g * (1.0 / (1.0 + jnp.exp(-g)))


def _norm_swiglu_up_kernel(x_ref, g_ref, wg_ref, wu_ref, o_ref, xn_ref):
    @pl.when(pl.program_id(1) == 0)
    def _():
        xn_ref[...] = _rms_normalize(x_ref[...], g_ref[...]).astype(BF16)

    xn = xn_ref[...]
    gate = jnp.dot(xn, wg_ref[...], preferred_element_type=F32)
    up = jnp.dot(xn, wu_ref[...], preferred_element_type=F32)
    o_ref[...] = (_silu(gate) * up).astype(o_ref.dtype)


def norm_swiglu_up(x, g, wg, wu):
    n, d = x.shape
    f = wg.shape[1]
    tm, tf = ROW_TILE, FFN_COL_TILE
    return pl.pallas_call(
        _norm_swiglu_up_kernel,
        out_shape=jax.ShapeDtypeStruct((n, f), BF16),
        grid=(n // tm, f // tf),
        in_specs=[
            pl.BlockSpec((tm, d), lambda i, j: (i, 0)),
            pl.BlockSpec((1, d), lambda i, j: (0, 0)),
            pl.BlockSpec((d, tf), lambda i, j: (0, j)),
            pl.BlockSpec((d, tf), lambda i, j: (0, j)),
        ],
        out_specs=pl.BlockSpec((tm, tf), lambda i, j: (i, j)),
        scratch_shapes=[pltpu.VMEM((tm, d), BF16)],
        compiler_params=_params("parallel", "arbitrary"),
        name="norm_swiglu_up",
    )(x, g.reshape(1, d), wg, wu)


def _matmul_residual_kernel(a_ref, w_ref, r_ref, o_ref):
    o_ref[...] = r_ref[...] + jnp.dot(a_ref[...], w_ref[...], preferred_element_type=F32)


def matmul_residual(a, w, res):
    n, k = a.shape
    cols = w.shape[1]
    tm, tn = ROW_TILE, min(COL_TILE, cols)
    while 2 * (2 * k * (tm + tn) + 8 * tm * tn) + 4 * tm * tn > VMEM_LIMIT_BYTES:
        tn //= 2
    return pl.pallas_call(
        _matmul_residual_kernel,
        out_shape=jax.ShapeDtypeStruct((n, cols), F32),
        grid=(n // tm, cols // tn),
        in_specs=[
            pl.BlockSpec((tm, k), lambda i, j: (i, 0)),
            pl.BlockSpec((k, tn), lambda i, j: (0, j)),
            pl.BlockSpec((tm, tn), lambda i, j: (i, j)),
        ],
        out_specs=pl.BlockSpec((tm, tn), lambda i, j: (i, j)),
        compiler_params=_params("parallel", "arbitrary"),
        name="matmul_residual",
    )(a, w, res)


_N_GAMMA_PARTS = 3
_OFF_LANE = 8
_BLK_LANE = _OFF_LANE + _N_GAMMA_PARTS


def _key_aug_table(seq):
    pos = jnp.arange(seq, dtype=jnp.int32)[:, None]
    lane = jnp.arange(HEAD_DIM, dtype=jnp.int32)[None, :]
    off = (pos % ATT_BLOCK).astype(F32)
    base = (pos - pos % ATT_BLOCK).astype(F32)
    tab = jnp.where(lane == pos // ATT_BLOCK, 1.0, 0.0)
    tab = jnp.where((lane >= _OFF_LANE) & (lane < _BLK_LANE), off, tab)
    tab = jnp.where((lane >= _BLK_LANE) & (lane < _BLK_LANE + _N_GAMMA_PARTS), base, tab)
    return tab.astype(BF16)


def _query_aug_rows(n_heads):
    hh = jnp.arange(1, n_heads + 1, dtype=F32)
    gamma = jnp.exp2(-8.0 * hh / n_heads) * (HEAD_DIM ** 0.5)
    parts = []
    rest = gamma
    for _ in range(_N_GAMMA_PARTS):
        part = rest.astype(BF16).astype(F32)
        parts.append(part)
        rest = rest - part
    lane = jnp.arange(HEAD_DIM, dtype=jnp.int32)[None, :]
    rows = jnp.zeros((n_heads, HEAD_DIM), F32)
    for p, part in enumerate(parts):
        rows = jnp.where((lane == _OFF_LANE + p) | (lane == _BLK_LANE + p), part[:, None], rows)
    return rows.reshape(n_heads, 1, HEAD_DIM)


_EXP2_SCALE = (HEAD_DIM ** -0.5) * math.log2(math.e)


def _causal_mask():
    blk = ATT_BLOCK
    return (lax.broadcasted_iota(jnp.int32, (blk, blk), 0)
            >= lax.broadcasted_iota(jnp.int32, (blk, blk), 1))


def _block_softmax(q_aug, kaug_ref, i, tri):
    blk = ATT_BLOCK
    z_own = lax.dot_general(q_aug, kaug_ref[i * blk:(i + 1) * blk, :], _NT, preferred_element_type=F32)
    z_own = jnp.where(tri, z_own, NEG_INF)
    m = jnp.max(z_own, axis=1, keepdims=True)
    if i == 0:
        p_own = jnp.exp2((z_own - m) * _EXP2_SCALE)
        return None, p_own, jnp.sum(p_own, axis=1, keepdims=True)
    z_past = lax.dot_general(q_aug, kaug_ref[:i * blk, :], _NT, preferred_element_type=F32)
    m = jnp.maximum(m, jnp.max(z_past, axis=1, keepdims=True))
    p_own = jnp.exp2((z_own - m) * _EXP2_SCALE)
    p_past = jnp.exp2((z_past - m) * _EXP2_SCALE)
    l = jnp.sum(p_own, axis=1, keepdims=True) + jnp.sum(p_past, axis=1, keepdims=True)
    return p_past, p_own, l


def _moba_kernel(q_ref, k_ref, v_ref, ktab_ref, qrow_ref, o_ref, kaug_ref, kmean_ref, *, n_blk):
    blk, hd = ATT_BLOCK, HEAD_DIM
    kaug_ref[:, :hd] = k_ref[...]
    kaug_ref[:, hd:] = ktab_ref[...]
    for j in range(n_blk):
        kj = k_ref[j * blk:(j + 1) * blk, :].astype(F32)
        kmean_ref[j:j + 1, :] = jnp.mean(kj, axis=0, keepdims=True)

    km = kmean_ref[...]
    km_hi = km.astype(BF16)
    km_lo = (km - km_hi.astype(F32)).astype(BF16)
    q_all = q_ref[...]
    gate_t = (lax.dot_general(km_hi, q_all, _NT, preferred_element_type=F32)
              + lax.dot_general(km_lo, q_all, _NT, preferred_element_type=F32))

    tri = _causal_mask()
    eye = (lax.broadcasted_iota(jnp.int32, (blk, blk), 0)
           == lax.broadcasted_iota(jnp.int32, (blk, blk), 1)).astype(BF16)
    blk_id = lax.broadcasted_iota(jnp.int32, (n_blk, blk), 0)
    q_row = qrow_ref[...]

    for i in range(n_blk):
        aug = jnp.broadcast_to(q_row, (blk, hd))
        if i > MOBA_TOPK:
            g = gate_t[:, i * blk:(i + 1) * blk]
            cnt = jnp.zeros(g.shape, F32)
            for jp in range(i):
                row = g[jp:jp + 1, :]
                beats = (row > g) | ((row == g) & (jp < blk_id))
                cnt = cnt + jnp.where(beats, 1.0, 0.0)
            drop_t = jnp.where((cnt >= float(MOBA_TOPK)) & (blk_id < i), 1.0, 0.0)
            drop_t = jnp.concatenate([drop_t, jnp.zeros((hd - n_blk, blk), F32)], axis=0)
            drop = lax.dot_general(eye, drop_t.astype(BF16), _NT, preferred_element_type=F32)
            aug = aug + drop * NEG_INF
        q_aug = jnp.concatenate([q_ref[i * blk:(i + 1) * blk, :], aug.astype(BF16)], axis=1)
        p_past, p_own, l = _block_softmax(q_aug, kaug_ref, i, tri)
        o = jnp.dot(p_own.astype(BF16), v_ref[i * blk:(i + 1) * blk, :], preferred_element_type=F32)
        if i > 0:
            o = o + jnp.dot(p_past.astype(BF16), v_ref[:i * blk, :], preferred_element_type=F32)
        o_ref[i * blk:(i + 1) * blk, :] = (o / l).astype(o_ref.dtype)


def moba_attention(qkv, bsz, seq, n_heads):
    n = bsz * seq
    blk, hd = ATT_BLOCK, HEAD_DIM
    n_blk = seq // blk
    return pl.pallas_call(
        functools.partial(_moba_kernel, n_blk=n_blk),
        out_shape=jax.ShapeDtypeStruct((n, n_heads * hd), BF16),
        grid=(bsz, n_heads),
        in_specs=[
            pl.BlockSpec((seq, hd), lambda b, h: (b, h)),
            pl.BlockSpec((seq, hd), lambda b, h: (b, n_heads + h)),
            pl.BlockSpec((seq, hd), lambda b, h: (b, 2 * n_heads + h)),
            pl.BlockSpec((seq, hd), lambda b, h: (0, 0)),
            pl.BlockSpec((None, 1, hd), lambda b, h: (h, 0, 0)),
        ],
        out_specs=pl.BlockSpec((seq, hd), lambda b, h: (b, h)),
        scratch_shapes=[
            pltpu.VMEM((seq, 2 * hd), BF16),
            pltpu.VMEM((n_blk, hd), F32),
        ],
        compiler_params=_params("parallel", "parallel"),
        name="moba_attention",
    )(qkv, qkv, qkv, _key_aug_table(seq), _query_aug_rows(n_heads))


def _diff_kernel(q_ref, k_ref, v_ref, ktab_ref, qrow_ref, lq1_ref, lk1_ref, lq2_ref, lk2_ref, sg_ref,
                 o_ref, qaug_ref, kaug_ref, *, n_blk, lam_init):
    blk, hd = ATT_BLOCK, HEAD_DIM
    seq = n_blk * blk
    q_row = jnp.broadcast_to(qrow_ref[...], (seq, hd)).astype(BF16)
    for c in range(2):
        qaug_ref[c, :, :hd] = q_ref[:, c * hd:(c + 1) * hd]
        qaug_ref[c, :, hd:] = q_row
        kaug_ref[c, :, :hd] = k_ref[:, c * hd:(c + 1) * hd]
        kaug_ref[c, :, hd:] = ktab_ref[...]

    lam = (jnp.exp(jnp.sum(lq1_ref[...] * lk1_ref[...], axis=1, keepdims=True))
           - jnp.exp(jnp.sum(lq2_ref[...] * lk2_ref[...], axis=1, keepdims=True))
           + lam_init)
    tri = _causal_mask()

    for i in range(n_blk):
        rows = slice(i * blk, (i + 1) * blk)
        p1_past, p1_own, l1 = _block_softmax(qaug_ref[0, rows, :], kaug_ref.at[0], i, tri)
        p2_past, p2_own, l2 = _block_softmax(qaug_ref[1, rows, :], kaug_ref.at[1], i, tri)
        r1 = 1.0 / l1
        r2 = lam / l2
        w_own = (p1_own * r1 - p2_own * r2).astype(BF16)
        o = jnp.dot(w_own, v_ref[rows, :], preferred_element_type=F32)
        if i > 0:
            w_past = (p1_past * r1 - p2_past * r2).astype(BF16)
            o = o + jnp.dot(w_past, v_ref[:i * blk, :], preferred_element_type=F32)
        o = _rms_normalize(o, sg_ref[...]) * (1.0 - lam_init)
        o_ref[rows, :] = o.astype(o_ref.dtype)


def diff_attention(q, kv, lq1, lk1, lq2, lk2, subln_g, lam_init, bsz, seq, n_heads):
    n = bsz * seq
    blk, hd = ATT_BLOCK, HEAD_DIM
    n_blk = seq // blk
    vec = lambda a: a.reshape(1, hd)
    vec_spec = pl.BlockSpec((1, hd), lambda b, h: (0, 0))
    return pl.pallas_call(
        functools.partial(_diff_kernel, n_blk=n_blk, lam_init=lam_init),
        out_shape=jax.ShapeDtypeStruct((n, n_heads * 2 * hd), BF16),
        grid=(bsz, n_heads),
        in_specs=[
            pl.BlockSpec((seq, 2 * hd), lambda b, h: (b, h)),
            pl.BlockSpec((seq, 2 * hd), lambda b, h: (b, h)),
            pl.BlockSpec((seq, 2 * hd), lambda b, h: (b, n_heads + h)),
            pl.BlockSpec((seq, hd), lambda b, h: (0, 0)),
            pl.BlockSpec((None, 1, hd), lambda b, h: (h, 0, 0)),
            vec_spec, vec_spec, vec_spec, vec_spec,
            pl.BlockSpec((1, 2 * hd), lambda b, h: (0, 0)),
        ],
        out_specs=pl.BlockSpec((seq, 2 * hd), lambda b, h: (b, h)),
        scratch_shapes=[
            pltpu.VMEM((2, seq, 2 * hd), BF16),
            pltpu.VMEM((2, seq, 2 * hd), BF16),
        ],
        compiler_params=_params("parallel", "parallel"),
        name="diff_attention",
    )(q, kv, kv, _key_aug_table(seq), _query_aug_rows(n_heads),
      vec(lq1), vec(lk1), vec(lq2), vec(lk2), subln_g.reshape(1, 2 * hd))


def _router_kernel(x_ref, g_ref, w_ref, o_ref):
    hn = _rms_normalize(x_ref[...], g_ref[...])
    w = w_ref[...]
    h_hi = hn.astype(BF16)
    h_lo = (hn - h_hi.astype(F32)).astype(BF16)
    w_hi = w.astype(BF16)
    w_lo = (w - w_hi.astype(F32)).astype(BF16)
    logits = (jnp.dot(h_hi, w_hi, preferred_element_type=F32)
              + jnp.dot(h_hi, w_lo, preferred_element_type=F32)
              + jnp.dot(h_lo, w_hi, preferred_element_type=F32))
    lane = lax.broadcasted_iota(jnp.int32, logits.shape, 1).astype(F32)
    big = float(N_EXPERTS)
    m1 = jnp.max(logits, axis=1, keepdims=True)
    e1 = jnp.min(jnp.where(logits == m1, lane, big), axis=1, keepdims=True)
    rest = jnp.where(lane == e1, -jnp.inf, logits)
    m2 = jnp.max(rest, axis=1, keepdims=True)
    e2 = jnp.min(jnp.where(rest == m2, lane, big), axis=1, keepdims=True)
    t = jnp.exp(m2 - m1)
    g1 = 1.0 / (1.0 + t)
    g2 = t / (1.0 + t)
    o_ref[...] = jnp.where(lane == 0.0, e1,
                 jnp.where(lane == 1.0, e2,
                 jnp.where(lane == 2.0, g1,
                 jnp.where(lane == 3.0, g2, 0.0))))


def router(x, g, w_router):
    n, d = x.shape
    tm = ROUTER_TILE
    return pl.pallas_call(
        _router_kernel,
        out_shape=jax.ShapeDtypeStruct((n, N_EXPERTS), F32),
        grid=(n // tm,),
        in_specs=[
            pl.BlockSpec((tm, d), lambda i: (i, 0)),
            pl.BlockSpec((1, d), lambda i: (0, 0)),
            pl.BlockSpec((d, N_EXPERTS), lambda i: (0, 0)),
        ],
        out_specs=pl.BlockSpec((tm, N_EXPERTS), lambda i: (i, 0)),
        compiler_params=_params("parallel"),
        name="router",
    )(x, g.reshape(1, d), w_router)


def _dispatch_kernel(row_tok_ref, nvalid_ref, x_hbm, g_ref, o_ref, buf_ref, sem):
    t = pl.program_id(0)
    tm = buf_ref.shape[0]

    @pl.when(t < nvalid_ref[0])
    def _():
        def issue(r, carry):
            tok = row_tok_ref[t * tm + r]
            pltpu.make_async_copy(x_hbm.at[pl.ds(tok, 1), :], buf_ref.at[pl.ds(r, 1), :], sem).start()
            return carry
        lax.fori_loop(0, tm, issue, 0)
        pltpu.make_async_copy(x_hbm.at[pl.ds(0, tm), :], buf_ref, sem).wait()
        o_ref[...] = _rms_normalize(buf_ref[...], g_ref[...]).astype(o_ref.dtype)

    @pl.when(t >= nvalid_ref[0])
    def _():
        o_ref[...] = jnp.zeros(o_ref.shape, o_ref.dtype)


def dispatch(x, g, row_tok, n_valid_tiles, n_rows):
    n, d = x.shape
    tm = MOE_ROW_TILE
    return pl.pallas_call(
        _dispatch_kernel,
        out_shape=jax.ShapeDtypeStruct((n_rows, d), BF16),
        grid_spec=pltpu.PrefetchScalarGridSpec(
            num_scalar_prefetch=2,
            grid=(n_rows // tm,),
            in_specs=[
                pl.BlockSpec(memory_space=pl.ANY),
                pl.BlockSpec((1, d), lambda t, rt, nv: (0, 0)),
            ],
            out_specs=pl.BlockSpec((tm, d), lambda t, rt, nv: (t, 0)),
            scratch_shapes=[pltpu.VMEM((tm, d), F32), pltpu.SemaphoreType.DMA],
        ),
        compiler_params=_params("arbitrary"),
        name="moe_dispatch",
    )(row_tok, n_valid_tiles, x, g.reshape(1, d))


def _expert_kernel(tile_e_ref, nvalid_ref, xs_ref, wg_ref, wu_ref, wd_ref, o_ref):
    t = pl.program_id(0)
    f = pl.program_id(1)

    @pl.when(f == 0)
    def _():
        o_ref[...] = jnp.zeros(o_ref.shape, o_ref.dtype)

    @pl.when(t < nvalid_ref[0])
    def _():
        xs = xs_ref[...]
        gate = jnp.dot(xs, wg_ref[...], preferred_element_type=F32)
        up = jnp.dot(xs, wu_ref[...], preferred_element_type=F32)
        act = (_silu(gate) * up).astype(BF16)
        o_ref[...] += jnp.dot(act, wd_ref[...], preferred_element_type=F32)


def expert_ffn(xs, wg, wu, wd, tile_e, n_valid_tiles):
    n_rows, d = xs.shape
    fe = wg.shape[2]
    tm, tf = MOE_ROW_TILE, MOE_F_TILE
    nf = fe // tf

    def f_idx(t, f, nv):
        return jnp.where(t < nv[0], f, nf - 1)

    return pl.pallas_call(
        _expert_kernel,
        out_shape=jax.ShapeDtypeStruct((n_rows, d), F32),
        grid_spec=pltpu.PrefetchScalarGridSpec(
            num_scalar_prefetch=2,
            grid=(n_rows // tm, nf),
            in_specs=[
                pl.BlockSpec((tm, d), lambda t, f, te, nv: (t, 0)),
                pl.BlockSpec((None, d, tf), lambda t, f, te, nv: (te[t], 0, f_idx(t, f, nv))),
                pl.BlockSpec((None, d, tf), lambda t, f, te, nv: (te[t], 0, f_idx(t, f, nv))),
                pl.BlockSpec((None, tf, d), lambda t, f, te, nv: (te[t], f_idx(t, f, nv), 0)),
            ],
            out_specs=pl.BlockSpec((tm, d), lambda t, f, te, nv: (t, 0)),
        ),
        compiler_params=_params("arbitrary", "arbitrary"),
        name="moe_experts",
    )(tile_e, n_valid_tiles, xs, wg, wu, wd)


def _combine_kernel(dest_ref, x_ref, gates_ref, g_ref, ys_hbm, o_ref, buf_ref, sem):
    t = pl.program_id(0)
    tm = x_ref.shape[0]

    def issue(r, carry):
        for k in range(TOP_K):
            row = dest_ref[(t * tm + r) * TOP_K + k]
            pltpu.make_async_copy(ys_hbm.at[pl.ds(row, 1), :], buf_ref.at[k, pl.ds(r, 1), :], sem).start()
        return carry
    lax.fori_loop(0, tm, issue, 0)
    pltpu.make_async_copy(buf_ref, buf_ref, sem).wait()
    gates = gates_ref[...]
    y = x_ref[...] + gates[:, 2:3] * buf_ref[0] + gates[:, 3:4] * buf_ref[1]
    o_ref[...] = _rms_normalize(y, g_ref[...])


def combine(x, gates, final_g, ys, dest):
    n, d = x.shape
    tm = COMBINE_TILE
    return pl.pallas_call(
        _combine_kernel,
        out_shape=jax.ShapeDtypeStruct((n, d), F32),
        grid_spec=pltpu.PrefetchScalarGridSpec(
            num_scalar_prefetch=1,
            grid=(n // tm,),
            in_specs=[
                pl.BlockSpec((tm, d), lambda t, dst: (t, 0)),
                pl.BlockSpec((tm, N_EXPERTS), lambda t, dst: (t, 0)),
                pl.BlockSpec((1, d), lambda t, dst: (0, 0)),
                pl.BlockSpec(memory_space=pl.ANY),
            ],
            out_specs=pl.BlockSpec((tm, d), lambda t, dst: (t, 0)),
            scratch_shapes=[pltpu.VMEM((TOP_K, tm, d), F32), pltpu.SemaphoreType.DMA],
        ),
        compiler_params=_params("arbitrary"),
        name="moe_combine",
    )(dest, x, gates, final_g.reshape(1, d), ys)


def _routing_tables(gates_out, n_tok):
    tm = MOE_ROW_TILE
    n_asg = n_tok * TOP_K
    n_rows = n_asg + N_EXPERTS * tm
    n_tiles = n_rows // tm
    flat_e = gates_out[:, :TOP_K].astype(jnp.int32).reshape(-1)
    onehot = (flat_e[:, None] == jnp.arange(N_EXPERTS, dtype=jnp.int32)[None, :]).astype(jnp.int32)
    csum = jnp.cumsum(onehot, axis=0)
    rank = jnp.sum(csum * onehot, axis=1) - 1
    counts = csum[-1]
    padded = ((counts + tm - 1) // tm) * tm
    pend = jnp.cumsum(padded)
    pstart = pend - padded
    dest = (jnp.sum(pstart[None, :] * onehot, axis=1) + rank).astype(jnp.int32)
    row_tok = jnp.zeros((n_rows,), jnp.int32).at[dest].set(
        jnp.arange(n_asg, dtype=jnp.int32) // TOP_K)
    n_valid = (pend[-1] // tm).astype(jnp.int32)
    tile_start = jnp.minimum(jnp.arange(n_tiles, dtype=jnp.int32), n_valid - 1) * tm
    tile_e = jnp.minimum(jnp.sum((pend[None, :] <= tile_start[:, None]).astype(jnp.int32), axis=1),
                         N_EXPERTS - 1).astype(jnp.int32)
    return dest, row_tok, tile_e, n_valid.reshape(1), n_rows


def kernel(x, attn_a_norm_g, w_qkv_a, w_o_a, kv_norm_g, w_kv, attn_b_norm_g, w_q_b,
           lam_q1, lam_k1, lam_q2, lam_k2, subln_g, w_o_b, ffn_norm_g, w_gate, w_up,
           w_down, w_router, w_exp_gate, w_exp_up, w_exp_down, final_norm_g):
    bsz, seq, d = x.shape
    n = bsz * seq
    n_heads_a = d // HEAD_DIM
    n_heads_b = d // (2 * HEAD_DIM)
    assert ffn_norm_g.shape[0] == 2 and attn_a_norm_g.shape[0] == 1 and attn_b_norm_g.shape[0] == 1
    bf = lambda w: w.astype(BF16)

    xf = x.reshape(n, d)

    qkv = norm_matmul(xf, attn_a_norm_g[0], bf(w_qkv_a[0]))
    o = moba_attention(qkv, bsz, seq, n_heads_a)
    xf = matmul_residual(o, bf(w_o_a[0]), xf)
    act = norm_swiglu_up(xf, ffn_norm_g[0], bf(w_gate[0]), bf(w_up[0]))
    xf = matmul_residual(act, bf(w_down[0]), xf)

    kv = norm_matmul(xf, kv_norm_g, bf(w_kv))
    q = norm_matmul(xf, attn_b_norm_g[0], bf(w_q_b[0]))
    lam_init = 0.8 - 0.6 * math.exp(-0.3 * 1)
    o = diff_attention(q, kv, lam_q1[0], lam_k1[0], lam_q2[0], lam_k2[0],
                       subln_g[0], lam_init, bsz, seq, n_heads_b)
    xf = matmul_residual(o, bf(w_o_b[0]), xf)
    gates_out = router(xf, ffn_norm_g[1], w_router[0])
    dest, row_tok, tile_e, n_valid, n_rows = _routing_tables(gates_out, n)
    xs = dispatch(xf, ffn_norm_g[1], row_tok, n_valid, n_rows)
    ys = expert_ffn(xs, bf(w_exp_gate[0]), bf(w_exp_up[0]), bf(w_exp_down[0]), tile_e, n_valid)
    out = combine(xf, gates_out, final_norm_g, ys, dest)
    return out.reshape(bsz, seq, d)
```

```python
import functools
import math

import jax
import jax.numpy as jnp
from jax import lax
from jax.experimental import pallas as pl
from jax.experimental.pallas import tpu as pltpu

F32 = jnp.float32
BF16 = jnp.bfloat16

HEAD_DIM = 128
MOBA_BLOCK = 256
MOBA_TOPK = 3
N_EXPERTS = 8
TOP_K = 2
RMS_EPS = 1e-6
NEG_INF = -1e30

VMEM_LIMIT_BYTES = 56 * 1024 * 1024

ROW_TILE = 1024
COL_TILE = 1024
FFN_COL_TILE = 512
ATT_BLOCK = 256
MOE_SUB_TILE = 256
MOE_TILE_SUBS = 5
MOE_F_TILE = 256
ROUTER_TILE = 512
COMBINE_TILE = 256

_NT = (((1,), (1,)), ((), ()))


def _params(*sem):
    return pltpu.CompilerParams(dimension_semantics=sem, vmem_limit_bytes=VMEM_LIMIT_BYTES)


def _rms_normalize(x, g):
    ms = jnp.mean(x * x, axis=-1, keepdims=True)
    return (x * lax.rsqrt(ms + RMS_EPS)) * g


def _norm_matmul_kernel(x_ref, g_ref, w_ref, o_ref, xn_ref):
    @pl.when(pl.program_id(1) == 0)
    def _():
        xn_ref[...] = _rms_normalize(x_ref[...], g_ref[...]).astype(BF16)

    w = w_ref[...].astype(BF16)
    o_ref[...] = jnp.dot(xn_ref[...], w, preferred_element_type=F32).astype(o_ref.dtype)


def norm_matmul(x, g, w, out_dtype=BF16):
    n, d = x.shape
    cols = w.shape[1]
    tm, tn = ROW_TILE, min(COL_TILE, cols)
    return pl.pallas_call(
        _norm_matmul_kernel,
        out_shape=jax.ShapeDtypeStruct((n, cols), out_dtype),
        grid=(n // tm, cols // tn),
        in_specs=[
            pl.BlockSpec((tm, d), lambda i, j: (i, 0)),
            pl.BlockSpec((1, d), lambda i, j: (0, 0)),
            pl.BlockSpec((d, tn), lambda i, j: (0, j)),
        ],
        out_specs=pl.BlockSpec((tm, tn), lambda i, j: (i, j)),
        scratch_shapes=[pltpu.VMEM((tm, d), BF16)],
        compiler_params=_params("parallel", "arbitrary"),
        name="norm_matmul",
    )(x, g.reshape(1, d), w)


def _silu(g):
    return g * (1.0 / (1.0 + jnp.exp(-g)))


def _norm_swiglu_up_kernel(x_ref, g_ref, wg_ref, wu_ref, o_ref, xn_ref):
    @pl.when(pl.program_id(1) == 0)
    def _():
        xn_ref[...] = _rms_normalize(x_ref[...], g_ref[...]).astype(BF16)

    xn = xn_ref[...]
    gate = jnp.dot(xn, wg_ref[...].astype(BF16), preferred_element_type=F32)
    up = jnp.dot(xn, wu_ref[...].astype(BF16), preferred_element_type=F32)
    o_ref[...] = (_silu(gate) * up).astype(o_ref.dtype)


def norm_swiglu_up(x, g, wg, wu):
    n, d = x.shape
    f = wg.shape[1]
    tm, tf = ROW_TILE, FFN_COL_TILE
    return pl.pallas_call(
        _norm_swiglu_up_kernel,
        out_shape=jax.ShapeDtypeStruct((n, f), BF16),
        grid=(n // tm, f // tf),
        in_specs=[
            pl.BlockSpec((tm, d), lambda i, j: (i, 0)),
            pl.BlockSpec((1, d), lambda i, j: (0, 0)),
            pl.BlockSpec((d, tf), lambda i, j: (0, j)),
            pl.BlockSpec((d, tf), lambda i, j: (0, j)),
        ],
        out_specs=pl.BlockSpec((tm, tf), lambda i, j: (i, j)),
        scratch_shapes=[pltpu.VMEM((tm, d), BF16)],
        compiler_params=_params("parallel", "arbitrary"),
        name="norm_swiglu_up",
    )(x, g.reshape(1, d), wg, wu)


def _matmul_residual_kernel(a_ref, w_ref, r_ref, o_ref):
    o_ref[...] = r_ref[...] + jnp.dot(a_ref[...], w_ref[...], preferred_element_type=F32)


def matmul_residual(a, w, res):
    n, k = a.shape
    cols = w.shape[1]
    tm, tn = ROW_TILE, min(COL_TILE, cols)
    while 2 * (2 * k * (tm + tn) + 8 * tm * tn) + 4 * tm * tn > VMEM_LIMIT_BYTES:
        tn //= 2
    return pl.pallas_call(
        _matmul_residual_kernel,
        out_shape=jax.ShapeDtypeStruct((n, cols), F32),
        grid=(n // tm, cols // tn),
        in_specs=[
            pl.BlockSpec((tm, k), lambda i, j: (i, 0)),
            pl.BlockSpec((k, tn), lambda i, j: (0, j)),
            pl.BlockSpec((tm, tn), lambda i, j: (i, j)),
        ],
        out_specs=pl.BlockSpec((tm, tn), lambda i, j: (i, j)),
        compiler_params=_params("parallel", "arbitrary"),
        name="matmul_residual",
    )(a, w, res)


_N_GAMMA_PARTS = 3
_OFF_LANE = 8
_BLK_LANE = _OFF_LANE + _N_GAMMA_PARTS


def _key_aug_table(seq):
    pos = jnp.arange(seq, dtype=jnp.int32)[:, None]
    lane = jnp.arange(HEAD_DIM, dtype=jnp.int32)[None, :]
    off = (pos % ATT_BLOCK).astype(F32)
    base = (pos - pos % ATT_BLOCK).astype(F32)
    tab = jnp.where(lane == pos // ATT_BLOCK, 1.0, 0.0)
    tab = jnp.where((lane >= _OFF_LANE) & (lane < _BLK_LANE), off, tab)
    tab = jnp.where((lane >= _BLK_LANE) & (lane < _BLK_LANE + _N_GAMMA_PARTS), base, tab)
    return tab.astype(BF16)


def _query_aug_rows(n_heads):
    hh = jnp.arange(1, n_heads + 1, dtype=F32)
    gamma = jnp.exp2(-8.0 * hh / n_heads) * (HEAD_DIM ** 0.5)
    parts = []
    rest = gamma
    for _ in range(_N_GAMMA_PARTS):
        part = rest.astype(BF16).astype(F32)
        parts.append(part)
        rest = rest - part
    lane = jnp.arange(HEAD_DIM, dtype=jnp.int32)[None, :]
    rows = jnp.zeros((n_heads, HEAD_DIM), F32)
    for p, part in enumerate(parts):
        rows = jnp.where((lane == _OFF_LANE + p) | (lane == _BLK_LANE + p), part[:, None], rows)
    return rows.reshape(n_heads, 1, HEAD_DIM)


_EXP2_SCALE = (HEAD_DIM ** -0.5) * math.log2(math.e)


def _causal_mask():
    blk = ATT_BLOCK
    return (lax.broadcasted_iota(jnp.int32, (blk, blk), 0)
            >= lax.broadcasted_iota(jnp.int32, (blk, blk), 1))


def _block_softmax(q_aug, kaug_ref, i, tri):
    blk = ATT_BLOCK
    z_own = lax.dot_general(q_aug, kaug_ref[i * blk:(i + 1) * blk, :], _NT, preferred_element_type=F32)
    z_own = jnp.where(tri, z_own, NEG_INF)
    m = jnp.max(z_own, axis=1, keepdims=True)
    if i == 0:
        p_own = jnp.exp2((z_own - m) * _EXP2_SCALE)
        return None, p_own, jnp.sum(p_own, axis=1, keepdims=True)
    z_past = lax.dot_general(q_aug, kaug_ref[:i * blk, :], _NT, preferred_element_type=F32)
    m = jnp.maximum(m, jnp.max(z_past, axis=1, keepdims=True))
    p_own = jnp.exp2((z_own - m) * _EXP2_SCALE)
    p_past = jnp.exp2((z_past - m) * _EXP2_SCALE)
    l = jnp.sum(p_own, axis=1, keepdims=True) + jnp.sum(p_past, axis=1, keepdims=True)
    return p_past, p_own, l


def _moba_kernel(q_ref, k_ref, v_ref, ktab_ref, qrow_ref, o_ref, kaug_ref, kmean_ref, *, n_blk):
    blk, hd = ATT_BLOCK, HEAD_DIM
    kaug_ref[:, :hd] = k_ref[...]
    kaug_ref[:, hd:] = ktab_ref[...]
    for j in range(n_blk):
        kj = k_ref[j * blk:(j + 1) * blk, :].astype(F32)
        kmean_ref[j:j + 1, :] = jnp.mean(kj, axis=0, keepdims=True)

    km = kmean_ref[...]
    km_hi = km.astype(BF16)
    km_lo = (km - km_hi.astype(F32)).astype(BF16)
    q_all = q_ref[...]
    gate_t = (lax.dot_general(km_hi, q_all, _NT, preferred_element_type=F32)
              + lax.dot_general(km_lo, q_all, _NT, preferred_element_type=F32))

    tri = _causal_mask()
    eye = (lax.broadcasted_iota(jnp.int32, (blk, blk), 0)
           == lax.broadcasted_iota(jnp.int32, (blk, blk), 1)).astype(BF16)
    blk_id = lax.broadcasted_iota(jnp.int32, (n_blk, blk), 0)
    q_row = qrow_ref[...]

    for i in range(n_blk):
        aug = jnp.broadcast_to(q_row, (blk, hd))
        if i > MOBA_TOPK:
            g = gate_t[:, i * blk:(i + 1) * blk]
            cnt = jnp.zeros(g.shape, F32)
            for jp in range(i):
                row = g[jp:jp + 1, :]
                beats = (row > g) | ((row == g) & (jp < blk_id))
                cnt = cnt + jnp.where(beats, 1.0, 0.0)
            drop_t = jnp.where((cnt >= float(MOBA_TOPK)) & (blk_id < i), 1.0, 0.0)
            drop_t = jnp.concatenate([drop_t, jnp.zeros((hd - n_blk, blk), F32)], axis=0)
            drop = lax.dot_general(eye, drop_t.astype(BF16), _NT, preferred_element_type=F32)
            aug = aug + drop * NEG_INF
        q_aug = jnp.concatenate([q_ref[i * blk:(i + 1) * blk, :], aug.astype(BF16)], axis=1)
        p_past, p_own, l = _block_softmax(q_aug, kaug_ref, i, tri)
        o = jnp.dot(p_own.astype(BF16), v_ref[i * blk:(i + 1) * blk, :], preferred_element_type=F32)
        if i > 0:
            o = o + jnp.dot(p_past.astype(BF16), v_ref[:i * blk, :], preferred_element_type=F32)
        o_ref[i * blk:(i + 1) * blk, :] = (o / l).astype(o_ref.dtype)


def moba_attention(qkv, bsz, seq, n_heads):
    n = bsz * seq
    blk, hd = ATT_BLOCK, HEAD_DIM
    n_blk = seq // blk
    return pl.pallas_call(
        functools.partial(_moba_kernel, n_blk=n_blk),
        out_shape=jax.ShapeDtypeStruct((n, n_heads * hd), BF16),
        grid=(bsz, n_heads),
        in_specs=[
            pl.BlockSpec((seq, hd), lambda b, h: (b, h)),
            pl.BlockSpec((seq, hd), lambda b, h: (b, n_heads + h)),
            pl.BlockSpec((seq, hd), lambda b, h: (b, 2 * n_heads + h)),
            pl.BlockSpec((seq, hd), lambda b, h: (0, 0)),
            pl.BlockSpec((None, 1, hd), lambda b, h: (h, 0, 0)),
        ],
        out_specs=pl.BlockSpec((seq, hd), lambda b, h: (b, h)),
        scratch_shapes=[
            pltpu.VMEM((seq, 2 * hd), BF16),
            pltpu.VMEM((n_blk, hd), F32),
        ],
        compiler_params=_params("parallel", "parallel"),
        name="moba_attention",
    )(qkv, qkv, qkv, _key_aug_table(seq), _query_aug_rows(n_heads))


def _diff_kernel(q_ref, k_ref, v_ref, ktab_ref, qrow_ref, lq1_ref, lk1_ref, lq2_ref, lk2_ref, sg_ref,
                 o_ref, qaug_ref, kaug_ref, *, n_blk, lam_init):
    blk, hd = ATT_BLOCK, HEAD_DIM
    seq = n_blk * blk
    q_row = jnp.broadcast_to(qrow_ref[...], (seq, hd)).astype(BF16)
    for c in range(2):
        qaug_ref[c, :, :hd] = q_ref[:, c * hd:(c + 1) * hd]
        qaug_ref[c, :, hd:] = q_row
        kaug_ref[c, :, :hd] = k_ref[:, c * hd:(c + 1) * hd]
        kaug_ref[c, :, hd:] = ktab_ref[...]

    lam = (jnp.exp(jnp.sum(lq1_ref[...] * lk1_ref[...], axis=1, keepdims=True))
           - jnp.exp(jnp.sum(lq2_ref[...] * lk2_ref[...], axis=1, keepdims=True))
           + lam_init)
    tri = _causal_mask()

    for i in range(n_blk):
        rows = slice(i * blk, (i + 1) * blk)
        p1_past, p1_own, l1 = _block_softmax(qaug_ref[0, rows, :], kaug_ref.at[0], i, tri)
        p2_past, p2_own, l2 = _block_softmax(qaug_ref[1, rows, :], kaug_ref.at[1], i, tri)
        r1 = 1.0 / l1
        r2 = lam / l2
        w_own = (p1_own * r1 - p2_own * r2).astype(BF16)
        o = jnp.dot(w_own, v_ref[rows, :], preferred_element_type=F32)
        if i > 0:
            w_past = (p1_past * r1 - p2_past * r2).astype(BF16)
            o = o + jnp.dot(w_past, v_ref[:i * blk, :], preferred_element_type=F32)
        o = _rms_normalize(o, sg_ref[...]) * (1.0 - lam_init)
        o_ref[rows, :] = o.astype(o_ref.dtype)


def diff_attention(q, kv, lq1, lk1, lq2, lk2, subln_g, lam_init, bsz, seq, n_heads):
    n = bsz * seq
    blk, hd = ATT_BLOCK, HEAD_DIM
    n_blk = seq // blk
    vec = lambda a: a.reshape(1, hd)
    vec_spec = pl.BlockSpec((1, hd), lambda b, h: (0, 0))
    return pl.pallas_call(
        functools.partial(_diff_kernel, n_blk=n_blk, lam_init=lam_init),
        out_shape=jax.ShapeDtypeStruct((n, n_heads * 2 * hd), BF16),
        grid=(bsz, n_heads),
        in_specs=[
            pl.BlockSpec((seq, 2 * hd), lambda b, h: (b, h)),
            pl.BlockSpec((seq, 2 * hd), lambda b, h: (b, h)),
            pl.BlockSpec((seq, 2 * hd), lambda b, h: (b, n_heads + h)),
            pl.BlockSpec((seq, hd), lambda b, h: (0, 0)),
            pl.BlockSpec((None, 1, hd), lambda b, h: (h, 0, 0)),
            vec_spec, vec_spec, vec_spec, vec_spec,
            pl.BlockSpec((1, 2 * hd), lambda b, h: (0, 0)),
        ],
        out_specs=pl.BlockSpec((seq, 2 * hd), lambda b, h: (b, h)),
        scratch_shapes=[
            pltpu.VMEM((2, seq, 2 * hd), BF16),
            pltpu.VMEM((2, seq, 2 * hd), BF16),
        ],
        compiler_params=_params("parallel", "parallel"),
        name="diff_attention",
    )(q, kv, kv, _key_aug_table(seq), _query_aug_rows(n_heads),
      vec(lq1), vec(lk1), vec(lq2), vec(lk2), subln_g.reshape(1, 2 * hd))


def _router_kernel(x_ref, g_ref, w_ref, o_ref):
    hn = _rms_normalize(x_ref[...], g_ref[...])
    w = w_ref[...]
    h_hi = hn.astype(BF16)
    h_lo = (hn - h_hi.astype(F32)).astype(BF16)
    w_hi = w.astype(BF16)
    w_lo = (w - w_hi.astype(F32)).astype(BF16)
    logits = (jnp.dot(h_hi, w_hi, preferred_element_type=F32)
              + jnp.dot(h_hi, w_lo, preferred_element_type=F32)
              + jnp.dot(h_lo, w_hi, preferred_element_type=F32))
    lane = lax.broadcasted_iota(jnp.int32, logits.shape, 1).astype(F32)
    big = float(N_EXPERTS)
    m1 = jnp.max(logits, axis=1, keepdims=True)
    e1 = jnp.min(jnp.where(logits == m1, lane, big), axis=1, keepdims=True)
    rest = jnp.where(lane == e1, -jnp.inf, logits)
    m2 = jnp.max(rest, axis=1, keepdims=True)
    e2 = jnp.min(jnp.where(rest == m2, lane, big), axis=1, keepdims=True)
    t = jnp.exp(m2 - m1)
    g1 = 1.0 / (1.0 + t)
    g2 = t / (1.0 + t)
    o_ref[...] = jnp.where(lane == 0.0, e1,
                 jnp.where(lane == 1.0, e2,
                 jnp.where(lane == 2.0, g1,
                 jnp.where(lane == 3.0, g2, 0.0))))


def router(x, g, w_router):
    n, d = x.shape
    tm = ROUTER_TILE
    return pl.pallas_call(
        _router_kernel,
        out_shape=jax.ShapeDtypeStruct((n, N_EXPERTS), F32),
        grid=(n // tm,),
        in_specs=[
            pl.BlockSpec((tm, d), lambda i: (i, 0)),
            pl.BlockSpec((1, d), lambda i: (0, 0)),
            pl.BlockSpec((d, N_EXPERTS), lambda i: (0, 0)),
        ],
        out_specs=pl.BlockSpec((tm, N_EXPERTS), lambda i: (i, 0)),
        compiler_params=_params("parallel"),
        name="router",
    )(x, g.reshape(1, d), w_router)


def _dispatch_kernel(row_tok_ref, sub_valid_ref, x_hbm, g_ref, o_ref, buf_ref, sem):
    t = pl.program_id(0)
    tm = buf_ref.shape[0]
    valid = sub_valid_ref[t] > 0

    @pl.when(valid)
    def _():
        def issue(r, carry):
            tok = row_tok_ref[t * tm + r]
            pltpu.make_async_copy(x_hbm.at[pl.ds(tok, 1), :], buf_ref.at[pl.ds(r, 1), :], sem).start()
            return carry
        lax.fori_loop(0, tm, issue, 0)
        pltpu.make_async_copy(x_hbm.at[pl.ds(0, tm), :], buf_ref, sem).wait()
        o_ref[...] = _rms_normalize(buf_ref[...], g_ref[...]).astype(o_ref.dtype)

    @pl.when(jnp.logical_not(valid))
    def _():
        o_ref[...] = jnp.zeros(o_ref.shape, o_ref.dtype)


def dispatch(x, g, row_tok, sub_valid, n_rows):
    n, d = x.shape
    tm = MOE_SUB_TILE
    return pl.pallas_call(
        _dispatch_kernel,
        out_shape=jax.ShapeDtypeStruct((n_rows, d), BF16),
        grid_spec=pltpu.PrefetchScalarGridSpec(
            num_scalar_prefetch=2,
            grid=(n_rows // tm,),
            in_specs=[
                pl.BlockSpec(memory_space=pl.ANY),
                pl.BlockSpec((1, d), lambda t, rt, nv: (0, 0)),
            ],
            out_specs=pl.BlockSpec((tm, d), lambda t, rt, nv: (t, 0)),
            scratch_shapes=[pltpu.VMEM((tm, d), F32), pltpu.SemaphoreType.DMA],
        ),
        compiler_params=_params("arbitrary"),
        name="moe_dispatch",
    )(row_tok, sub_valid, x, g.reshape(1, d))


def _expert_kernel(tile_e_ref, tile_rows_ref, xs_ref, wg_ref, wu_ref, wd_ref, o_ref):
    t = pl.program_id(0)
    f = pl.program_id(1)
    n_sub = tile_rows_ref[t]

    @pl.when(f == 0)
    def _():
        o_ref[...] = jnp.zeros(o_ref.shape, o_ref.dtype)

    def swiglu(rows):
        xs = xs_ref[rows, :]
        gate = jnp.dot(xs, wg_ref[...].astype(BF16), preferred_element_type=F32)
        up = jnp.dot(xs, wu_ref[...].astype(BF16), preferred_element_type=F32)
        act = (_silu(gate) * up).astype(BF16)
        o_ref[rows, :] += jnp.dot(act, wd_ref[...].astype(BF16), preferred_element_type=F32)

    for occupied in range(1, MOE_TILE_SUBS + 1):
        @pl.when(n_sub == occupied)
        def _(occupied=occupied):
            swiglu(slice(0, occupied * MOE_SUB_TILE))


def expert_ffn(xs, wg, wu, wd, tile_e, tile_rows):
    n_rows, d = xs.shape
    fe = wg.shape[2]
    tm, tf = MOE_TILE_SUBS * MOE_SUB_TILE, MOE_F_TILE
    nf = fe // tf

    def f_idx(t, f, tr):
        return jnp.where(tr[t] > 0, f, nf - 1)

    return pl.pallas_call(
        _expert_kernel,
        out_shape=jax.ShapeDtypeStruct((n_rows, d), F32),
        grid_spec=pltpu.PrefetchScalarGridSpec(
            num_scalar_prefetch=2,
            grid=(n_rows // tm, nf),
            in_specs=[
                pl.BlockSpec((tm, d), lambda t, f, te, tr: (t, 0)),
                pl.BlockSpec((None, d, tf), lambda t, f, te, tr: (te[t], 0, f_idx(t, f, tr))),
                pl.BlockSpec((None, d, tf), lambda t, f, te, tr: (te[t], 0, f_idx(t, f, tr))),
                pl.BlockSpec((None, tf, d), lambda t, f, te, tr: (te[t], f_idx(t, f, tr), 0)),
            ],
            out_specs=pl.BlockSpec((tm, d), lambda t, f, te, tr: (t, 0)),
        ),
        compiler_params=_params("arbitrary", "arbitrary"),
        name="moe_experts",
    )(tile_e, tile_rows, xs, wg, wu, wd)


def _combine_kernel(dest_ref, x_ref, gates_ref, g_ref, ys_hbm, o_ref, buf_ref, sem):
    t = pl.program_id(0)
    tm = x_ref.shape[0]

    def issue(r, carry):
        for k in range(TOP_K):
            row = dest_ref[(t * tm + r) * TOP_K + k]
            pltpu.make_async_copy(ys_hbm.at[pl.ds(row, 1), :], buf_ref.at[k, pl.ds(r, 1), :], sem).start()
        return carry
    lax.fori_loop(0, tm, issue, 0)
    pltpu.make_async_copy(buf_ref, buf_ref, sem).wait()
    gates = gates_ref[...]
    y = x_ref[...] + gates[:, 2:3] * buf_ref[0] + gates[:, 3:4] * buf_ref[1]
    o_ref[...] = _rms_normalize(y, g_ref[...])


def combine(x, gates, final_g, ys, dest):
    n, d = x.shape
    tm = COMBINE_TILE
    return pl.pallas_call(
        _combine_kernel,
        out_shape=jax.ShapeDtypeStruct((n, d), F32),
        grid_spec=pltpu.PrefetchScalarGridSpec(
            num_scalar_prefetch=1,
            grid=(n // tm,),
            in_specs=[
                pl.BlockSpec((tm, d), lambda t, dst: (t, 0)),
                pl.BlockSpec((tm, N_EXPERTS), lambda t, dst: (t, 0)),
                pl.BlockSpec((1, d), lambda t, dst: (0, 0)),
                pl.BlockSpec(memory_space=pl.ANY),
            ],
            out_specs=pl.BlockSpec((tm, d), lambda t, dst: (t, 0)),
            scratch_shapes=[pltpu.VMEM((TOP_K, tm, d), F32), pltpu.SemaphoreType.DMA],
        ),
        compiler_params=_params("arbitrary"),
        name="moe_combine",
    )(dest, x, gates, final_g.reshape(1, d), ys)


def _routing_tables(gates_out, n_tok):
    sub, slots = MOE_SUB_TILE, MOE_TILE_SUBS
    tm = slots * sub
    n_asg = n_tok * TOP_K
    max_subs = n_asg // sub + N_EXPERTS
    n_tiles = (max_subs + N_EXPERTS * (slots - 1)) // slots
    n_rows = n_tiles * tm
    i32 = jnp.int32
    flat_e = gates_out[:, :TOP_K].astype(i32).reshape(-1)
    onehot = (flat_e[:, None] == jnp.arange(N_EXPERTS, dtype=i32)[None, :]).astype(i32)
    csum = jnp.cumsum(onehot, axis=0)
    rank = jnp.sum(csum * onehot, axis=1) - 1
    counts = csum[-1]
    subs = (counts + sub - 1) // sub
    tiles = (subs + slots - 1) // slots
    base = subs // jnp.maximum(tiles, 1)
    extra = subs - base * tiles
    tile_end = jnp.cumsum(tiles)
    tile_start = tile_end - tiles

    pick = lambda per_expert: jnp.sum(per_expert[None, :] * onehot, axis=1)
    a_base, a_extra, a_start = pick(base), pick(extra), pick(tile_start)
    q = rank // sub
    big = a_extra * (a_base + 1)
    small_base = jnp.maximum(a_base, 1)
    k = jnp.where(q < big, q // (a_base + 1), a_extra + (q - big) // small_base)
    slot = jnp.where(q < big, q % (a_base + 1), (q - big) % small_base)
    dest = ((a_start + k) * tm + slot * sub + rank % sub).astype(i32)
    row_tok = jnp.zeros((n_rows,), i32).at[dest].set(jnp.arange(n_asg, dtype=i32) // TOP_K)

    n_used = tile_end[-1]
    tile_id = jnp.minimum(jnp.arange(n_tiles, dtype=i32), n_used - 1)
    tile_e = jnp.minimum(jnp.sum((tile_end[None, :] <= tile_id[:, None]).astype(i32), axis=1),
                         N_EXPERTS - 1).astype(i32)
    k_tile = tile_id - tile_start[tile_e]
    occupied = base[tile_e] + (k_tile < extra[tile_e]).astype(i32)
    tile_rows = jnp.where(jnp.arange(n_tiles, dtype=i32) < n_used, occupied, 0).astype(i32)
    sub_valid = (jnp.arange(slots, dtype=i32)[None, :] < tile_rows[:, None]).astype(i32)
    return dest, row_tok, tile_e, tile_rows, sub_valid.reshape(-1), n_rows


def kernel(x, attn_a_norm_g, w_qkv_a, w_o_a, kv_norm_g, w_kv, attn_b_norm_g, w_q_b,
           lam_q1, lam_k1, lam_q2, lam_k2, subln_g, w_o_b, ffn_norm_g, w_gate, w_up,
           w_down, w_router, w_exp_gate, w_exp_up, w_exp_down, final_norm_g):
    bsz, seq, d = x.shape
    n = bsz * seq
    n_heads_a = d // HEAD_DIM
    n_heads_b = d // (2 * HEAD_DIM)
    assert ffn_norm_g.shape[0] == 2 and attn_a_norm_g.shape[0] == 1 and attn_b_norm_g.shape[0] == 1
    bf = lambda w: w.astype(BF16)

    xf = x.reshape(n, d)

    qkv = norm_matmul(xf, attn_a_norm_g[0], w_qkv_a[0])
    o = moba_attention(qkv, bsz, seq, n_heads_a)
    xf = matmul_residual(o, bf(w_o_a[0]), xf)
    act = norm_swiglu_up(xf, ffn_norm_g[0], w_gate[0], w_up[0])
    xf = matmul_residual(act, bf(w_down[0]), xf)

    kv = norm_matmul(xf, kv_norm_g, w_kv)
    q = norm_matmul(xf, attn_b_norm_g[0], bf(w_q_b[0]))
    lam_init = 0.8 - 0.6 * math.exp(-0.3 * 1)
    o = diff_attention(q, kv, lam_q1[0], lam_k1[0], lam_q2[0], lam_k2[0],
                       subln_g[0], lam_init, bsz, seq, n_heads_b)
    xf = matmul_residual(o, bf(w_o_b[0]), xf)
    gates_out = router(xf, ffn_norm_g[1], w_router[0])
    dest, row_tok, tile_e, tile_rows, sub_valid, n_rows = _routing_tables(gates_out, n)
    xs = dispatch(xf, ffn_norm_g[1], row_tok, sub_valid, n_rows)
    ys = expert_ffn(xs, w_exp_gate[0], w_exp_up[0], w_exp_down[0], tile_e, tile_rows)
    out = combine(xf, gates_out, final_norm_g, ys, dest)
    return out.reshape(bsz, seq, d)
```

```python
import functools
import math

import jax
import jax.numpy as jnp
from jax import lax
from jax.experimental import pallas as pl
from jax.experimental.pallas import tpu as pltpu

F32 = jnp.float32
BF16 = jnp.bfloat16

HEAD_DIM = 128
MOBA_BLOCK = 256
MOBA_TOPK = 3
N_EXPERTS = 8
TOP_K = 2
RMS_EPS = 1e-6
NEG_INF = -1e30

VMEM_LIMIT_BYTES = 56 * 1024 * 1024

ROW_TILE = 1024
COL_TILE = 1024
FFN_COL_TILE = 512
ATT_BLOCK = 256
MOE_SUB_TILE = 256
MOE_TILE_SUBS = 5
MOE_F_TILE = 256
ROUTER_TILE = 512
COMBINE_TILE = 256

_NT = (((1,), (1,)), ((), ()))


def _params(*sem):
    return pltpu.CompilerParams(dimension_semantics=sem, vmem_limit_bytes=VMEM_LIMIT_BYTES)


def _rms_normalize(x, g):
    ms = jnp.mean(x * x, axis=-1, keepdims=True)
    return (x * lax.rsqrt(ms + RMS_EPS)) * g


def _norm_matmul_kernel(x_ref, g_ref, w_ref, o_ref, xn_ref):
    @pl.when(pl.program_id(1) == 0)
    def _():
        xn_ref[...] = _rms_normalize(x_ref[...], g_ref[...]).astype(BF16)

    w = w_ref[...].astype(BF16)
    res = jnp.dot(xn_ref[...], w, preferred_element_type=F32).astype(o_ref.dtype)
    n_groups, _, width = o_ref.shape
    for c in range(n_groups):
        o_ref[c] = res[:, c * width:(c + 1) * width]


def norm_matmul(x, g, w, width):
    n, d = x.shape
    cols = w.shape[1]
    tm, tn = ROW_TILE, min(COL_TILE, cols)
    return pl.pallas_call(
        _norm_matmul_kernel,
        out_shape=jax.ShapeDtypeStruct((cols // width, n, width), BF16),
        grid=(n // tm, cols // tn),
        in_specs=[
            pl.BlockSpec((tm, d), lambda i, j: (i, 0)),
            pl.BlockSpec((1, d), lambda i, j: (0, 0)),
            pl.BlockSpec((d, tn), lambda i, j: (0, j)),
        ],
        out_specs=pl.BlockSpec((tn // width, tm, width), lambda i, j: (j, i, 0)),
        scratch_shapes=[pltpu.VMEM((tm, d), BF16)],
        compiler_params=_params("parallel", "arbitrary"),
        name="norm_matmul",
    )(x, g.reshape(1, d), w)


def _silu(g):
    return g * (1.0 / (1.0 + jnp.exp(-g)))


def _norm_swiglu_up_kernel(x_ref, g_ref, wg_ref, wu_ref, o_ref, xn_ref):
    @pl.when(pl.program_id(1) == 0)
    def _():
        xn_ref[...] = _rms_normalize(x_ref[...], g_ref[...]).astype(BF16)

    xn = xn_ref[...]
    gate = jnp.dot(xn, wg_ref[...].astype(BF16), preferred_element_type=F32)
    up = jnp.dot(xn, wu_ref[...].astype(BF16), preferred_element_type=F32)
    o_ref[...] = (_silu(gate) * up).astype(o_ref.dtype)


def norm_swiglu_up(x, g, wg, wu):
    n, d = x.shape
    f = wg.shape[1]
    tm, tf = ROW_TILE, FFN_COL_TILE
    return pl.pallas_call(
        _norm_swiglu_up_kernel,
        out_shape=jax.ShapeDtypeStruct((n, f), BF16),
        grid=(n // tm, f // tf),
        in_specs=[
            pl.BlockSpec((tm, d), lambda i, j: (i, 0)),
            pl.BlockSpec((1, d), lambda i, j: (0, 0)),
            pl.BlockSpec((d, tf), lambda i, j: (0, j)),
            pl.BlockSpec((d, tf), lambda i, j: (0, j)),
        ],
        out_specs=pl.BlockSpec((tm, tf), lambda i, j: (i, j)),
        scratch_shapes=[pltpu.VMEM((tm, d), BF16)],
        compiler_params=_params("parallel", "arbitrary"),
        name="norm_swiglu_up",
    )(x, g.reshape(1, d), wg, wu)


def _matmul_residual_kernel(a_ref, w_ref, r_ref, o_ref):
    o_ref[...] = r_ref[...] + jnp.dot(a_ref[...], w_ref[...], preferred_element_type=F32)


def _grouped_matmul_residual_kernel(a_ref, w_ref, r_ref, o_ref):
    a = jnp.concatenate([a_ref[c] for c in range(a_ref.shape[0])], axis=1)
    o_ref[...] = r_ref[...] + jnp.dot(a, w_ref[...], preferred_element_type=F32)


def grouped_matmul_residual(a, w, res):
    groups, n, width = a.shape
    k = groups * width
    cols = w.shape[1]
    tm, tn = ROW_TILE, min(COL_TILE, cols)
    return pl.pallas_call(
        _grouped_matmul_residual_kernel,
        out_shape=jax.ShapeDtypeStruct((n, cols), F32),
        grid=(n // tm, cols // tn),
        in_specs=[
            pl.BlockSpec((groups, tm, width), lambda i, j: (0, i, 0)),
            pl.BlockSpec((k, tn), lambda i, j: (0, j)),
            pl.BlockSpec((tm, tn), lambda i, j: (i, j)),
        ],
        out_specs=pl.BlockSpec((tm, tn), lambda i, j: (i, j)),
        compiler_params=_params("parallel", "arbitrary"),
        name="grouped_matmul_residual",
    )(a, w, res)


def matmul_residual(a, w, res):
    n, k = a.shape
    cols = w.shape[1]
    tm, tn = ROW_TILE, min(COL_TILE, cols)
    while 2 * (2 * k * (tm + tn) + 8 * tm * tn) + 4 * tm * tn > VMEM_LIMIT_BYTES:
        tn //= 2
    return pl.pallas_call(
        _matmul_residual_kernel,
        out_shape=jax.ShapeDtypeStruct((n, cols), F32),
        grid=(n // tm, cols // tn),
        in_specs=[
            pl.BlockSpec((tm, k), lambda i, j: (i, 0)),
            pl.BlockSpec((k, tn), lambda i, j: (0, j)),
            pl.BlockSpec((tm, tn), lambda i, j: (i, j)),
        ],
        out_specs=pl.BlockSpec((tm, tn), lambda i, j: (i, j)),
        compiler_params=_params("parallel", "arbitrary"),
        name="matmul_residual",
    )(a, w, res)


_N_GAMMA_PARTS = 3
_OFF_LANE = 8
_BLK_LANE = _OFF_LANE + _N_GAMMA_PARTS


def _key_aug_table(seq):
    pos = jnp.arange(seq, dtype=jnp.int32)[:, None]
    lane = jnp.arange(HEAD_DIM, dtype=jnp.int32)[None, :]
    off = (pos % ATT_BLOCK).astype(F32)
    base = (pos - pos % ATT_BLOCK).astype(F32)
    tab = jnp.where(lane == pos // ATT_BLOCK, 1.0, 0.0)
    tab = jnp.where((lane >= _OFF_LANE) & (lane < _BLK_LANE), off, tab)
    tab = jnp.where((lane >= _BLK_LANE) & (lane < _BLK_LANE + _N_GAMMA_PARTS), base, tab)
    return tab.astype(BF16)


def _query_aug_rows(n_heads):
    hh = jnp.arange(1, n_heads + 1, dtype=F32)
    gamma = jnp.exp2(-8.0 * hh / n_heads) * (HEAD_DIM ** 0.5)
    parts = []
    rest = gamma
    for _ in range(_N_GAMMA_PARTS):
        part = rest.astype(BF16).astype(F32)
        parts.append(part)
        rest = rest - part
    lane = jnp.arange(HEAD_DIM, dtype=jnp.int32)[None, :]
    rows = jnp.zeros((n_heads, HEAD_DIM), F32)
    for p, part in enumerate(parts):
        rows = jnp.where((lane == _OFF_LANE + p) | (lane == _BLK_LANE + p), part[:, None], rows)
    return rows.reshape(n_heads, 1, HEAD_DIM)


_EXP2_SCALE = (HEAD_DIM ** -0.5) * math.log2(math.e)


def _causal_mask():
    blk = ATT_BLOCK
    return (lax.broadcasted_iota(jnp.int32, (blk, blk), 0)
            >= lax.broadcasted_iota(jnp.int32, (blk, blk), 1))


def _block_softmax(q_aug, kaug_ref, i, tri):
    blk = ATT_BLOCK
    z_own = lax.dot_general(q_aug, kaug_ref[i * blk:(i + 1) * blk, :], _NT, preferred_element_type=F32)
    z_own = jnp.where(tri, z_own, NEG_INF)
    m = jnp.max(z_own, axis=1, keepdims=True)
    if i == 0:
        p_own = jnp.exp2((z_own - m) * _EXP2_SCALE)
        return None, p_own, jnp.sum(p_own, axis=1, keepdims=True)
    z_past = lax.dot_general(q_aug, kaug_ref[:i * blk, :], _NT, preferred_element_type=F32)
    m = jnp.maximum(m, jnp.max(z_past, axis=1, keepdims=True))
    p_own = jnp.exp2((z_own - m) * _EXP2_SCALE)
    p_past = jnp.exp2((z_past - m) * _EXP2_SCALE)
    l = jnp.sum(p_own, axis=1, keepdims=True) + jnp.sum(p_past, axis=1, keepdims=True)
    return p_past, p_own, l


def _moba_kernel(q_ref, k_ref, v_ref, ktab_ref, qrow_ref, o_ref, kaug_ref, kmean_ref, *, n_blk):
    blk, hd = ATT_BLOCK, HEAD_DIM
    kaug_ref[:, :hd] = k_ref[...]
    kaug_ref[:, hd:] = ktab_ref[...]
    for j in range(n_blk):
        kj = k_ref[j * blk:(j + 1) * blk, :].astype(F32)
        kmean_ref[j:j + 1, :] = jnp.mean(kj, axis=0, keepdims=True)

    km = kmean_ref[...]
    km_hi = km.astype(BF16)
    km_lo = (km - km_hi.astype(F32)).astype(BF16)
    q_all = q_ref[...]
    gate_t = (lax.dot_general(km_hi, q_all, _NT, preferred_element_type=F32)
              + lax.dot_general(km_lo, q_all, _NT, preferred_element_type=F32))

    tri = _causal_mask()
    eye = (lax.broadcasted_iota(jnp.int32, (blk, blk), 0)
           == lax.broadcasted_iota(jnp.int32, (blk, blk), 1)).astype(BF16)
    blk_id = lax.broadcasted_iota(jnp.int32, (n_blk, blk), 0)
    q_row = qrow_ref[...]

    for i in range(n_blk):
        aug = jnp.broadcast_to(q_row, (blk, hd))
        if i > MOBA_TOPK:
            g = gate_t[:, i * blk:(i + 1) * blk]
            cnt = jnp.zeros(g.shape, F32)
            for jp in range(i):
                row = g[jp:jp + 1, :]
                beats = (row > g) | ((row == g) & (jp < blk_id))
                cnt = cnt + jnp.where(beats, 1.0, 0.0)
            drop_t = jnp.where((cnt >= float(MOBA_TOPK)) & (blk_id < i), 1.0, 0.0)
            drop_t = jnp.concatenate([drop_t, jnp.zeros((hd - n_blk, blk), F32)], axis=0)
            drop = lax.dot_general(eye, drop_t.astype(BF16), _NT, preferred_element_type=F32)
            aug = aug + drop * NEG_INF
        q_aug = jnp.concatenate([q_ref[i * blk:(i + 1) * blk, :], aug.astype(BF16)], axis=1)
        p_past, p_own, l = _block_softmax(q_aug, kaug_ref, i, tri)
        o = jnp.dot(p_own.astype(BF16), v_ref[i * blk:(i + 1) * blk, :], preferred_element_type=F32)
        if i > 0:
            o = o + jnp.dot(p_past.astype(BF16), v_ref[:i * blk, :], preferred_element_type=F32)
        o_ref[i * blk:(i + 1) * blk, :] = (o / l).astype(o_ref.dtype)


def moba_attention(qkv, bsz, seq, n_heads):
    n = bsz * seq
    blk, hd = ATT_BLOCK, HEAD_DIM
    n_blk = seq // blk
    return pl.pallas_call(
        functools.partial(_moba_kernel, n_blk=n_blk),
        out_shape=jax.ShapeDtypeStruct((n_heads, n, hd), BF16),
        grid=(bsz, n_heads),
        in_specs=[
            pl.BlockSpec((None, seq, hd), lambda b, h: (h, b, 0)),
            pl.BlockSpec((None, seq, hd), lambda b, h: (n_heads + h, b, 0)),
            pl.BlockSpec((None, seq, hd), lambda b, h: (2 * n_heads + h, b, 0)),
            pl.BlockSpec((seq, hd), lambda b, h: (0, 0)),
            pl.BlockSpec((None, 1, hd), lambda b, h: (h, 0, 0)),
        ],
        out_specs=pl.BlockSpec((None, seq, hd), lambda b, h: (h, b, 0)),
        scratch_shapes=[
            pltpu.VMEM((seq, 2 * hd), BF16),
            pltpu.VMEM((n_blk, hd), F32),
        ],
        compiler_params=_params("parallel", "parallel"),
        name="moba_attention",
    )(qkv, qkv, qkv, _key_aug_table(seq), _query_aug_rows(n_heads))


def _diff_kernel(q_ref, k_ref, v_ref, ktab_ref, qrow_ref, lq1_ref, lk1_ref, lq2_ref, lk2_ref, sg_ref,
                 o_ref, qaug_ref, kaug_ref, *, n_blk, lam_init):
    blk, hd = ATT_BLOCK, HEAD_DIM
    seq = n_blk * blk
    q_row = jnp.broadcast_to(qrow_ref[...], (seq, hd)).astype(BF16)
    for c in range(2):
        qaug_ref[c, :, :hd] = q_ref[:, c * hd:(c + 1) * hd]
        qaug_ref[c, :, hd:] = q_row
        kaug_ref[c, :, :hd] = k_ref[:, c * hd:(c + 1) * hd]
        kaug_ref[c, :, hd:] = ktab_ref[...]

    lam = (jnp.exp(jnp.sum(lq1_ref[...] * lk1_ref[...], axis=1, keepdims=True))
           - jnp.exp(jnp.sum(lq2_ref[...] * lk2_ref[...], axis=1, keepdims=True))
           + lam_init)
    tri = _causal_mask()

    for i in range(n_blk):
        rows = slice(i * blk, (i + 1) * blk)
        p1_past, p1_own, l1 = _block_softmax(qaug_ref[0, rows, :], kaug_ref.at[0], i, tri)
        p2_past, p2_own, l2 = _block_softmax(qaug_ref[1, rows, :], kaug_ref.at[1], i, tri)
        r1 = 1.0 / l1
        r2 = lam / l2
        w_own = (p1_own * r1 - p2_own * r2).astype(BF16)
        o = jnp.dot(w_own, v_ref[rows, :], preferred_element_type=F32)
        if i > 0:
            w_past = (p1_past * r1 - p2_past * r2).astype(BF16)
            o = o + jnp.dot(w_past, v_ref[:i * blk, :], preferred_element_type=F32)
        o = _rms_normalize(o, sg_ref[...]) * (1.0 - lam_init)
        o_ref[rows, :] = o.astype(o_ref.dtype)


def diff_attention(q, kv, lq1, lk1, lq2, lk2, subln_g, lam_init, bsz, seq, n_heads):
    n = bsz * seq
    blk, hd = ATT_BLOCK, HEAD_DIM
    n_blk = seq // blk
    vec = lambda a: a.reshape(1, hd)
    vec_spec = pl.BlockSpec((1, hd), lambda b, h: (0, 0))
    return pl.pallas_call(
        functools.partial(_diff_kernel, n_blk=n_blk, lam_init=lam_init),
        out_shape=jax.ShapeDtypeStruct((n_heads, n, 2 * hd), BF16),
        grid=(bsz, n_heads),
        in_specs=[
            pl.BlockSpec((None, seq, 2 * hd), lambda b, h: (h, b, 0)),
            pl.BlockSpec((None, seq, 2 * hd), lambda b, h: (h, b, 0)),
            pl.BlockSpec((None, seq, 2 * hd), lambda b, h: (n_heads + h, b, 0)),
            pl.BlockSpec((seq, hd), lambda b, h: (0, 0)),
            pl.BlockSpec((None, 1, hd), lambda b, h: (h, 0, 0)),
            vec_spec, vec_spec, vec_spec, vec_spec,
            pl.BlockSpec((1, 2 * hd), lambda b, h: (0, 0)),
        ],
        out_specs=pl.BlockSpec((None, seq, 2 * hd), lambda b, h: (h, b, 0)),
        scratch_shapes=[
            pltpu.VMEM((2, seq, 2 * hd), BF16),
            pltpu.VMEM((2, seq, 2 * hd), BF16),
        ],
        compiler_params=_params("parallel", "parallel"),
        name="diff_attention",
    )(q, kv, kv, _key_aug_table(seq), _query_aug_rows(n_heads),
      vec(lq1), vec(lk1), vec(lq2), vec(lk2), subln_g.reshape(1, 2 * hd))


def _router_kernel(x_ref, g_ref, w_ref, o_ref):
    hn = _rms_normalize(x_ref[...], g_ref[...])
    w = w_ref[...]
    h_hi = hn.astype(BF16)
    h_lo = (hn - h_hi.astype(F32)).astype(BF16)
    w_hi = w.astype(BF16)
    w_lo = (w - w_hi.astype(F32)).astype(BF16)
    logits = (jnp.dot(h_hi, w_hi, preferred_element_type=F32)
              + jnp.dot(h_hi, w_lo, preferred_element_type=F32)
              + jnp.dot(h_lo, w_hi, preferred_element_type=F32))
    lane = lax.broadcasted_iota(jnp.int32, logits.shape, 1).astype(F32)
    big = float(N_EXPERTS)
    m1 = jnp.max(logits, axis=1, keepdims=True)
    e1 = jnp.min(jnp.where(logits == m1, lane, big), axis=1, keepdims=True)
    rest = jnp.where(lane == e1, -jnp.inf, logits)
    m2 = jnp.max(rest, axis=1, keepdims=True)
    e2 = jnp.min(jnp.where(rest == m2, lane, big), axis=1, keepdims=True)
    t = jnp.exp(m2 - m1)
    g1 = 1.0 / (1.0 + t)
    g2 = t / (1.0 + t)
    o_ref[...] = jnp.where(lane == 0.0, e1,
                 jnp.where(lane == 1.0, e2,
                 jnp.where(lane == 2.0, g1,
                 jnp.where(lane == 3.0, g2, 0.0))))


def router(x, g, w_router):
    n, d = x.shape
    tm = ROUTER_TILE
    return pl.pallas_call(
        _router_kernel,
        out_shape=jax.ShapeDtypeStruct((n, N_EXPERTS), F32),
        grid=(n // tm,),
        in_specs=[
            pl.BlockSpec((tm, d), lambda i: (i, 0)),
            pl.BlockSpec((1, d), lambda i: (0, 0)),
            pl.BlockSpec((d, N_EXPERTS), lambda i: (0, 0)),
        ],
        out_specs=pl.BlockSpec((tm, N_EXPERTS), lambda i: (i, 0)),
        compiler_params=_params("parallel"),
        name="router",
    )(x, g.reshape(1, d), w_router)


def _expert_kernel(tile_e_ref, tile_rows_ref, row_tok_ref, x_hbm, g_ref, wg_ref, wu_ref, wd_ref, o_ref,
                   stage_ref, xs_ref, sem, *, rows_per_step):
    t = pl.program_id(0)
    f = pl.program_id(1)
    n_t = pl.num_programs(0)
    nf = pl.num_programs(1)
    tm = xs_ref.shape[0]
    n_stage = stage_ref.shape[0]
    n_sub = tile_rows_ref[t]

    def row_copy(tile, r):
        tok = row_tok_ref[tile * tm + r]
        return pltpu.make_async_copy(x_hbm.at[pl.ds(tok, 1), :], stage_ref.at[pl.ds(r, 1), :], sem)

    def wait_stage():
        pltpu.make_async_copy(x_hbm.at[pl.ds(0, n_stage), :], stage_ref, sem).wait()

    @pl.when(f == 0)
    def _():
        @pl.when(t == 0)
        def _():
            def issue(r, carry):
                row_copy(0, r).start()
                return carry
            lax.fori_loop(0, n_stage, issue, 0)
        wait_stage()
        xs_ref[...] = _rms_normalize(stage_ref[:tm, :], g_ref[...]).astype(BF16)
        o_ref[...] = jnp.zeros(o_ref.shape, o_ref.dtype)

    nxt = jnp.minimum(t + 1, n_t - 1)

    def prefetch_rows():
        for u in range(rows_per_step):
            row_copy(nxt, f * rows_per_step + u).start()

    def swiglu(rows):
        xs = xs_ref[rows, :]
        gate = jnp.dot(xs, wg_ref[...].astype(BF16), preferred_element_type=F32)
        up = jnp.dot(xs, wu_ref[...].astype(BF16), preferred_element_type=F32)
        act = (_silu(gate) * up).astype(BF16)
        o_ref[rows, :] += jnp.dot(act, wd_ref[...].astype(BF16), preferred_element_type=F32)

    for occupied in range(MOE_TILE_SUBS + 1):
        @pl.when(n_sub == occupied)
        def _(occupied=occupied):
            prefetch_rows()
            if occupied:
                swiglu(slice(0, occupied * MOE_SUB_TILE))

    @pl.when((t == n_t - 1) & (f == nf - 1))
    def _():
        wait_stage()


def expert_ffn(x, g, wg, wu, wd, tile_e, tile_rows, row_tok, n_rows):
    n, d = x.shape
    fe = wg.shape[2]
    tm, tf = MOE_TILE_SUBS * MOE_SUB_TILE, MOE_F_TILE
    nf = fe // tf
    rows_per_step = -(-tm // nf)
    n_stage = rows_per_step * nf
    row_tok = jnp.concatenate([row_tok, jnp.zeros((n_stage - tm,), jnp.int32)])

    def f_idx(t, f, tr):
        return jnp.where(tr[t] > 0, f, nf - 1)

    return pl.pallas_call(
        functools.partial(_expert_kernel, rows_per_step=rows_per_step),
        out_shape=jax.ShapeDtypeStruct((n_rows, d), F32),
        grid_spec=pltpu.PrefetchScalarGridSpec(
            num_scalar_prefetch=3,
            grid=(n_rows // tm, nf),
            in_specs=[
                pl.BlockSpec(memory_space=pl.ANY),
                pl.BlockSpec((1, d), lambda t, f, te, tr, rt: (0, 0)),
                pl.BlockSpec((None, d, tf), lambda t, f, te, tr, rt: (te[t], 0, f_idx(t, f, tr))),
                pl.BlockSpec((None, d, tf), lambda t, f, te, tr, rt: (te[t], 0, f_idx(t, f, tr))),
                pl.BlockSpec((None, tf, d), lambda t, f, te, tr, rt: (te[t], f_idx(t, f, tr), 0)),
            ],
            out_specs=pl.BlockSpec((tm, d), lambda t, f, te, tr, rt: (t, 0)),
            scratch_shapes=[
                pltpu.VMEM((n_stage, d), F32),
                pltpu.VMEM((tm, d), BF16),
                pltpu.SemaphoreType.DMA,
            ],
        ),
        compiler_params=_params("arbitrary", "arbitrary"),
        name="moe_experts",
    )(tile_e, tile_rows, row_tok, x, g.reshape(1, d), wg, wu, wd)


def _combine_kernel(dest_ref, x_ref, gates_ref, g_ref, ys_hbm, o_ref, buf_ref, sem):
    t = pl.program_id(0)
    tm = x_ref.shape[0]

    def issue(r, carry):
        for k in range(TOP_K):
            row = dest_ref[(t * tm + r) * TOP_K + k]
            pltpu.make_async_copy(ys_hbm.at[pl.ds(row, 1), :], buf_ref.at[k, pl.ds(r, 1), :], sem).start()
        return carry
    lax.fori_loop(0, tm, issue, 0)
    pltpu.make_async_copy(buf_ref, buf_ref, sem).wait()
    gates = gates_ref[...]
    y = x_ref[...] + gates[:, 2:3] * buf_ref[0] + gates[:, 3:4] * buf_ref[1]
    o_ref[...] = _rms_normalize(y, g_ref[...])


def combine(x, gates, final_g, ys, dest):
    n, d = x.shape
    tm = COMBINE_TILE
    return pl.pallas_call(
        _combine_kernel,
        out_shape=jax.ShapeDtypeStruct((n, d), F32),
        grid_spec=pltpu.PrefetchScalarGridSpec(
            num_scalar_prefetch=1,
            grid=(n // tm,),
            in_specs=[
                pl.BlockSpec((tm, d), lambda t, dst: (t, 0)),
                pl.BlockSpec((tm, N_EXPERTS), lambda t, dst: (t, 0)),
                pl.BlockSpec((1, d), lambda t, dst: (0, 0)),
                pl.BlockSpec(memory_space=pl.ANY),
            ],
            out_specs=pl.BlockSpec((tm, d), lambda t, dst: (t, 0)),
            scratch_shapes=[pltpu.VMEM((TOP_K, tm, d), F32), pltpu.SemaphoreType.DMA],
        ),
        compiler_params=_params("arbitrary"),
        name="moe_combine",
    )(dest, x, gates, final_g.reshape(1, d), ys)


def _routing_tables(gates_out, n_tok):
    sub, slots = MOE_SUB_TILE, MOE_TILE_SUBS
    tm = slots * sub
    n_asg = n_tok * TOP_K
    max_subs = n_asg // sub + N_EXPERTS
    n_tiles = (max_subs + N_EXPERTS * (slots - 1)) // slots
    n_rows = n_tiles * tm
    i32 = jnp.int32
    flat_e = gates_out[:, :TOP_K].astype(i32).reshape(-1)
    onehot = (flat_e[:, None] == jnp.arange(N_EXPERTS, dtype=i32)[None, :]).astype(i32)
    csum = jnp.cumsum(onehot, axis=0)
    rank = jnp.sum(csum * onehot, axis=1) - 1
    counts = csum[-1]
    subs = (counts + sub - 1) // sub
    tiles = (subs + slots - 1) // slots
    base = subs // jnp.maximum(tiles, 1)
    extra = subs - base * tiles
    tile_end = jnp.cumsum(tiles)
    tile_start = tile_end - tiles

    pick = lambda per_expert: jnp.sum(per_expert[None, :] * onehot, axis=1)
    a_base, a_extra, a_start = pick(base), pick(extra), pick(tile_start)
    q = rank // sub
    big = a_extra * (a_base + 1)
    small_base = jnp.maximum(a_base, 1)
    k = jnp.where(q < big, q // (a_base + 1), a_extra + (q - big) // small_base)
    slot = jnp.where(q < big, q % (a_base + 1), (q - big) % small_base)
    dest = ((a_start + k) * tm + slot * sub + rank % sub).astype(i32)
    row_tok = jnp.zeros((n_rows,), i32).at[dest].set(jnp.arange(n_asg, dtype=i32) // TOP_K)

    n_used = tile_end[-1]
    tile_id = jnp.minimum(jnp.arange(n_tiles, dtype=i32), n_used - 1)
    tile_e = jnp.minimum(jnp.sum((tile_end[None, :] <= tile_id[:, None]).astype(i32), axis=1),
                         N_EXPERTS - 1).astype(i32)
    k_tile = tile_id - tile_start[tile_e]
    occupied = base[tile_e] + (k_tile < extra[tile_e]).astype(i32)
    tile_rows = jnp.where(jnp.arange(n_tiles, dtype=i32) < n_used, occupied, 0).astype(i32)
    return dest, row_tok, tile_e, tile_rows, n_rows


def kernel(x, attn_a_norm_g, w_qkv_a, w_o_a, kv_norm_g, w_kv, attn_b_norm_g, w_q_b,
           lam_q1, lam_k1, lam_q2, lam_k2, subln_g, w_o_b, ffn_norm_g, w_gate, w_up,
           w_down, w_router, w_exp_gate, w_exp_up, w_exp_down, final_norm_g):
    bsz, seq, d = x.shape
    n = bsz * seq
    n_heads_a = d // HEAD_DIM
    n_heads_b = d // (2 * HEAD_DIM)
    assert ffn_norm_g.shape[0] == 2 and attn_a_norm_g.shape[0] == 1 and attn_b_norm_g.shape[0] == 1
    bf = lambda w: w.astype(BF16)

    xf = x.reshape(n, d)

    qkv = norm_matmul(xf, attn_a_norm_g[0], w_qkv_a[0], HEAD_DIM)
    o = moba_attention(qkv, bsz, seq, n_heads_a)
    xf = grouped_matmul_residual(o, bf(w_o_a[0]), xf)
    act = norm_swiglu_up(xf, ffn_norm_g[0], w_gate[0], w_up[0])
    xf = matmul_residual(act, bf(w_down[0]), xf)

    kv = norm_matmul(xf, kv_norm_g, w_kv, 2 * HEAD_DIM)
    q = norm_matmul(xf, attn_b_norm_g[0], bf(w_q_b[0]), 2 * HEAD_DIM)
    lam_init = 0.8 - 0.6 * math.exp(-0.3 * 1)
    o = diff_attention(q, kv, lam_q1[0], lam_k1[0], lam_q2[0], lam_k2[0],
                       subln_g[0], lam_init, bsz, seq, n_heads_b)
    xf = grouped_matmul_residual(o, bf(w_o_b[0]), xf)
    gates_out = router(xf, ffn_norm_g[1], w_router[0])
    dest, row_tok, tile_e, tile_rows, n_rows = _routing_tables(gates_out, n)
    ys = expert_ffn(xf, ffn_norm_g[1], w_exp_gate[0], w_exp_up[0], w_exp_down[0],
                    tile_e, tile_rows, row_tok, n_rows)
    out = combine(xf, gates_out, final_norm_g, ys, dest)
    return out.reshape(bsz, seq, d)
```

```python
import functools
import math

import jax
import jax.numpy as jnp
from jax import lax
from jax.experimental import pallas as pl
from jax.experimental.pallas import tpu as pltpu

F32 = jnp.float32
BF16 = jnp.bfloat16

HEAD_DIM = 128
MOBA_BLOCK = 256
MOBA_TOPK = 3
N_EXPERTS = 8
TOP_K = 2
RMS_EPS = 1e-6
NEG_INF = -1e30

VMEM_LIMIT_BYTES = 56 * 1024 * 1024

ROW_TILE = 1024
COL_TILE = 1024
FFN_COL_TILE = 512
ATT_BLOCK = 256
MOE_SUB_TILE = 256
MOE_TILE_SUBS = 5
MOE_F_TILE = 256
ROUTER_TILE = 512
COMBINE_TILE = 256

_NT = (((1,), (1,)), ((), ()))


def _params(*sem):
    return pltpu.CompilerParams(dimension_semantics=sem, vmem_limit_bytes=VMEM_LIMIT_BYTES)


def _rms_normalize(x, g):
    ms = jnp.mean(x * x, axis=-1, keepdims=True)
    return (x * lax.rsqrt(ms + RMS_EPS)) * g


def _norm_matmul_kernel(x_ref, g_ref, w_ref, o_ref, xn_ref):
    @pl.when(pl.program_id(1) == 0)
    def _():
        xn_ref[...] = _rms_normalize(x_ref[...], g_ref[...]).astype(BF16)

    w = w_ref[...].astype(BF16)
    res = jnp.dot(xn_ref[...], w, preferred_element_type=F32).astype(o_ref.dtype)
    n_groups, _, width = o_ref.shape
    for c in range(n_groups):
        o_ref[c] = res[:, c * width:(c + 1) * width]


def norm_matmul(x, g, w, width):
    n, d = x.shape
    cols = w.shape[1]
    tm, tn = ROW_TILE, min(COL_TILE, cols)
    return pl.pallas_call(
        _norm_matmul_kernel,
        out_shape=jax.ShapeDtypeStruct((cols // width, n, width), BF16),
        grid=(n // tm, cols // tn),
        in_specs=[
            pl.BlockSpec((tm, d), lambda i, j: (i, 0)),
            pl.BlockSpec((1, d), lambda i, j: (0, 0)),
            pl.BlockSpec((d, tn), lambda i, j: (0, j)),
        ],
        out_specs=pl.BlockSpec((tn // width, tm, width), lambda i, j: (j, i, 0)),
        scratch_shapes=[pltpu.VMEM((tm, d), BF16)],
        compiler_params=_params("parallel", "arbitrary"),
        name="norm_matmul",
    )(x, g.reshape(1, d), w)


def _silu(g):
    return g * (1.0 / (1.0 + jnp.exp(-g)))


def _norm_swiglu_up_kernel(x_ref, g_ref, wg_ref, wu_ref, o_ref, xn_ref):
    @pl.when(pl.program_id(1) == 0)
    def _():
        xn_ref[...] = _rms_normalize(x_ref[...], g_ref[...]).astype(BF16)

    xn = xn_ref[...]
    gate = jnp.dot(xn, wg_ref[...].astype(BF16), preferred_element_type=F32)
    up = jnp.dot(xn, wu_ref[...].astype(BF16), preferred_element_type=F32)
    o_ref[...] = (_silu(gate) * up).astype(o_ref.dtype)


def norm_swiglu_up(x, g, wg, wu):
    n, d = x.shape
    f = wg.shape[1]
    tm, tf = ROW_TILE, FFN_COL_TILE
    return pl.pallas_call(
        _norm_swiglu_up_kernel,
        out_shape=jax.ShapeDtypeStruct((n, f), BF16),
        grid=(n // tm, f // tf),
        in_specs=[
            pl.BlockSpec((tm, d), lambda i, j: (i, 0)),
            pl.BlockSpec((1, d), lambda i, j: (0, 0)),
            pl.BlockSpec((d, tf), lambda i, j: (0, j)),
            pl.BlockSpec((d, tf), lambda i, j: (0, j)),
        ],
        out_specs=pl.BlockSpec((tm, tf), lambda i, j: (i, j)),
        scratch_shapes=[pltpu.VMEM((tm, d), BF16)],
        compiler_params=_params("parallel", "arbitrary"),
        name="norm_swiglu_up",
    )(x, g.reshape(1, d), wg, wu)


def _matmul_residual_kernel(a_ref, w_ref, r_ref, o_ref):
    o_ref[...] = r_ref[...] + jnp.dot(a_ref[...], w_ref[...], preferred_element_type=F32)


def _grouped_matmul_residual_kernel(a_ref, w_ref, r_ref, o_ref):
    a = jnp.concatenate([a_ref[c] for c in range(a_ref.shape[0])], axis=1)
    o_ref[...] = r_ref[...] + jnp.dot(a, w_ref[...], preferred_element_type=F32)


def grouped_matmul_residual(a, w, res):
    groups, n, width = a.shape
    k = groups * width
    cols = w.shape[1]
    tm, tn = ROW_TILE, min(COL_TILE, cols)
    return pl.pallas_call(
        _grouped_matmul_residual_kernel,
        out_shape=jax.ShapeDtypeStruct((n, cols), F32),
        grid=(n // tm, cols // tn),
        in_specs=[
            pl.BlockSpec((groups, tm, width), lambda i, j: (0, i, 0)),
            pl.BlockSpec((k, tn), lambda i, j: (0, j)),
            pl.BlockSpec((tm, tn), lambda i, j: (i, j)),
        ],
        out_specs=pl.BlockSpec((tm, tn), lambda i, j: (i, j)),
        compiler_params=_params("parallel", "arbitrary"),
        name="grouped_matmul_residual",
    )(a, w, res)


def matmul_residual(a, w, res):
    n, k = a.shape
    cols = w.shape[1]
    tm, tn = ROW_TILE, min(COL_TILE, cols)
    while 2 * (2 * k * (tm + tn) + 8 * tm * tn) + 4 * tm * tn > VMEM_LIMIT_BYTES:
        tn //= 2
    return pl.pallas_call(
        _matmul_residual_kernel,
        out_shape=jax.ShapeDtypeStruct((n, cols), F32),
        grid=(n // tm, cols // tn),
        in_specs=[
            pl.BlockSpec((tm, k), lambda i, j: (i, 0)),
            pl.BlockSpec((k, tn), lambda i, j: (0, j)),
            pl.BlockSpec((tm, tn), lambda i, j: (i, j)),
        ],
        out_specs=pl.BlockSpec((tm, tn), lambda i, j: (i, j)),
        compiler_params=_params("parallel", "arbitrary"),
        name="matmul_residual",
    )(a, w, res)


_N_GAMMA_PARTS = 3
_OFF_LANE = 8
_BLK_LANE = _OFF_LANE + _N_GAMMA_PARTS


def _key_aug_table(seq):
    pos = jnp.arange(seq, dtype=jnp.int32)[:, None]
    lane = jnp.arange(HEAD_DIM, dtype=jnp.int32)[None, :]
    off = (pos % ATT_BLOCK).astype(F32)
    base = (pos - pos % ATT_BLOCK).astype(F32)
    tab = jnp.where(lane == pos // ATT_BLOCK, 1.0, 0.0)
    tab = jnp.where((lane >= _OFF_LANE) & (lane < _BLK_LANE), off, tab)
    tab = jnp.where((lane >= _BLK_LANE) & (lane < _BLK_LANE + _N_GAMMA_PARTS), base, tab)
    return tab.astype(BF16)


def _query_aug_rows(n_heads):
    hh = jnp.arange(1, n_heads + 1, dtype=F32)
    gamma = jnp.exp2(-8.0 * hh / n_heads) * (HEAD_DIM ** 0.5)
    parts = []
    rest = gamma
    for _ in range(_N_GAMMA_PARTS):
        part = rest.astype(BF16).astype(F32)
        parts.append(part)
        rest = rest - part
    lane = jnp.arange(HEAD_DIM, dtype=jnp.int32)[None, :]
    rows = jnp.zeros((n_heads, HEAD_DIM), F32)
    for p, part in enumerate(parts):
        rows = jnp.where((lane == _OFF_LANE + p) | (lane == _BLK_LANE + p), part[:, None], rows)
    return rows.reshape(n_heads, 1, HEAD_DIM)


_EXP2_SCALE = (HEAD_DIM ** -0.5) * math.log2(math.e)


def _causal_mask():
    blk = ATT_BLOCK
    return (lax.broadcasted_iota(jnp.int32, (blk, blk), 0)
            >= lax.broadcasted_iota(jnp.int32, (blk, blk), 1))


def _block_softmax(q_aug, kaug_ref, i, tri):
    blk = ATT_BLOCK
    z_own = lax.dot_general(q_aug, kaug_ref[i * blk:(i + 1) * blk, :], _NT, preferred_element_type=F32)
    z_own = jnp.where(tri, z_own, NEG_INF)
    m = jnp.max(z_own, axis=1, keepdims=True)
    if i == 0:
        p_own = jnp.exp2((z_own - m) * _EXP2_SCALE)
        return None, p_own, jnp.sum(p_own, axis=1, keepdims=True)
    z_past = lax.dot_general(q_aug, kaug_ref[:i * blk, :], _NT, preferred_element_type=F32)
    m = jnp.maximum(m, jnp.max(z_past, axis=1, keepdims=True))
    p_own = jnp.exp2((z_own - m) * _EXP2_SCALE)
    p_past = jnp.exp2((z_past - m) * _EXP2_SCALE)
    l = jnp.sum(p_own, axis=1, keepdims=True) + jnp.sum(p_past, axis=1, keepdims=True)
    return p_past, p_own, l


def _moba_kernel(q_ref, k_ref, v_ref, ktab_ref, qrow_ref, o_ref, kaug_ref, kmean_ref, *, n_blk):
    blk, hd = ATT_BLOCK, HEAD_DIM
    kaug_ref[:, :hd] = k_ref[...]
    kaug_ref[:, hd:] = ktab_ref[...]
    for j in range(n_blk):
        kj = k_ref[j * blk:(j + 1) * blk, :].astype(F32)
        kmean_ref[j:j + 1, :] = jnp.mean(kj, axis=0, keepdims=True)

    km = kmean_ref[...]
    km_hi = km.astype(BF16)
    km_lo = (km - km_hi.astype(F32)).astype(BF16)
    q_all = q_ref[...]
    gate_t = (lax.dot_general(km_hi, q_all, _NT, preferred_element_type=F32)
              + lax.dot_general(km_lo, q_all, _NT, preferred_element_type=F32))

    tri = _causal_mask()
    eye = (lax.broadcasted_iota(jnp.int32, (blk, blk), 0)
           == lax.broadcasted_iota(jnp.int32, (blk, blk), 1)).astype(BF16)
    blk_id = lax.broadcasted_iota(jnp.int32, (n_blk, blk), 0)
    q_row = qrow_ref[...]

    for i in range(n_blk):
        aug = jnp.broadcast_to(q_row, (blk, hd))
        if i > MOBA_TOPK:
            g = gate_t[:, i * blk:(i + 1) * blk]
            cnt = jnp.zeros(g.shape, F32)
            for jp in range(i):
                row = g[jp:jp + 1, :]
                beats = (row > g) | ((row == g) & (jp < blk_id))
                cnt = cnt + jnp.where(beats, 1.0, 0.0)
            drop_t = jnp.where((cnt >= float(MOBA_TOPK)) & (blk_id < i), 1.0, 0.0)
            drop_t = jnp.concatenate([drop_t, jnp.zeros((hd - n_blk, blk), F32)], axis=0)
            drop = lax.dot_general(eye, drop_t.astype(BF16), _NT, preferred_element_type=F32)
            aug = aug + drop * NEG_INF
        q_aug = jnp.concatenate([q_ref[i * blk:(i + 1) * blk, :], aug.astype(BF16)], axis=1)
        p_past, p_own, l = _block_softmax(q_aug, kaug_ref, i, tri)
        o = jnp.dot(p_own.astype(BF16), v_ref[i * blk:(i + 1) * blk, :], preferred_element_type=F32)
        if i > 0:
            o = o + jnp.dot(p_past.astype(BF16), v_ref[:i * blk, :], preferred_element_type=F32)
        o_ref[i * blk:(i + 1) * blk, :] = (o / l).astype(o_ref.dtype)


def moba_attention(qkv, bsz, seq, n_heads):
    n = bsz * seq
    blk, hd = ATT_BLOCK, HEAD_DIM
    n_blk = seq // blk
    return pl.pallas_call(
        functools.partial(_moba_kernel, n_blk=n_blk),
        out_shape=jax.ShapeDtypeStruct((n_heads, n, hd), BF16),
        grid=(bsz, n_heads),
        in_specs=[
            pl.BlockSpec((None, seq, hd), lambda b, h: (h, b, 0)),
            pl.BlockSpec((None, seq, hd), lambda b, h: (n_heads + h, b, 0)),
            pl.BlockSpec((None, seq, hd), lambda b, h: (2 * n_heads + h, b, 0)),
            pl.BlockSpec((seq, hd), lambda b, h: (0, 0)),
            pl.BlockSpec((None, 1, hd), lambda b, h: (h, 0, 0)),
        ],
        out_specs=pl.BlockSpec((None, seq, hd), lambda b, h: (h, b, 0)),
        scratch_shapes=[
            pltpu.VMEM((seq, 2 * hd), BF16),
            pltpu.VMEM((n_blk, hd), F32),
        ],
        compiler_params=_params("parallel", "parallel"),
        name="moba_attention",
    )(qkv, qkv, qkv, _key_aug_table(seq), _query_aug_rows(n_heads))


def _diff_kernel(q_ref, k_ref, v_ref, ktab_ref, qrow_ref, lq1_ref, lk1_ref, lq2_ref, lk2_ref, sg_ref,
                 o_ref, qaug_ref, kaug_ref, *, n_blk, lam_init):
    blk, hd = ATT_BLOCK, HEAD_DIM
    seq = n_blk * blk
    q_row = jnp.broadcast_to(qrow_ref[...], (seq, hd)).astype(BF16)
    for c in range(2):
        qaug_ref[c, :, :hd] = q_ref[:, c * hd:(c + 1) * hd]
        qaug_ref[c, :, hd:] = q_row
        kaug_ref[c, :, :hd] = k_ref[:, c * hd:(c + 1) * hd]
        kaug_ref[c, :, hd:] = ktab_ref[...]

    lam = (jnp.exp(jnp.sum(lq1_ref[...] * lk1_ref[...], axis=1, keepdims=True))
           - jnp.exp(jnp.sum(lq2_ref[...] * lk2_ref[...], axis=1, keepdims=True))
           + lam_init)
    tri = _causal_mask()

    for i in range(n_blk):
        rows = slice(i * blk, (i + 1) * blk)
        p1_past, p1_own, l1 = _block_softmax(qaug_ref[0, rows, :], kaug_ref.at[0], i, tri)
        p2_past, p2_own, l2 = _block_softmax(qaug_ref[1, rows, :], kaug_ref.at[1], i, tri)
        r1 = 1.0 / l1
        r2 = lam / l2
        w_own = (p1_own * r1 - p2_own * r2).astype(BF16)
        o = jnp.dot(w_own, v_ref[rows, :], preferred_element_type=F32)
        if i > 0:
            w_past = (p1_past * r1 - p2_past * r2).astype(BF16)
            o = o + jnp.dot(w_past, v_ref[:i * blk, :], preferred_element_type=F32)
        o = _rms_normalize(o, sg_ref[...]) * (1.0 - lam_init)
        o_ref[rows, :] = o.astype(o_ref.dtype)


def diff_attention(q, kv, lq1, lk1, lq2, lk2, subln_g, lam_init, bsz, seq, n_heads):
    n = bsz * seq
    blk, hd = ATT_BLOCK, HEAD_DIM
    n_blk = seq // blk
    vec = lambda a: a.reshape(1, hd)
    vec_spec = pl.BlockSpec((1, hd), lambda b, h: (0, 0))
    return pl.pallas_call(
        functools.partial(_diff_kernel, n_blk=n_blk, lam_init=lam_init),
        out_shape=jax.ShapeDtypeStruct((n_heads, n, 2 * hd), BF16),
        grid=(bsz, n_heads),
        in_specs=[
            pl.BlockSpec((None, seq, 2 * hd), lambda b, h: (h, b, 0)),
            pl.BlockSpec((None, seq, 2 * hd), lambda b, h: (h, b, 0)),
            pl.BlockSpec((None, seq, 2 * hd), lambda b, h: (n_heads + h, b, 0)),
            pl.BlockSpec((seq, hd), lambda b, h: (0, 0)),
            pl.BlockSpec((None, 1, hd), lambda b, h: (h, 0, 0)),
            vec_spec, vec_spec, vec_spec, vec_spec,
            pl.BlockSpec((1, 2 * hd), lambda b, h: (0, 0)),
        ],
        out_specs=pl.BlockSpec((None, seq, 2 * hd), lambda b, h: (h, b, 0)),
        scratch_shapes=[
            pltpu.VMEM((2, seq, 2 * hd), BF16),
            pltpu.VMEM((2, seq, 2 * hd), BF16),
        ],
        compiler_params=_params("parallel", "parallel"),
        name="diff_attention",
    )(q, kv, kv, _key_aug_table(seq), _query_aug_rows(n_heads),
      vec(lq1), vec(lk1), vec(lq2), vec(lk2), subln_g.reshape(1, 2 * hd))


def _router_kernel(x_ref, g_ref, w_ref, o_ref):
    hn = _rms_normalize(x_ref[...], g_ref[...])
    w = w_ref[...]
    h_hi = hn.astype(BF16)
    h_lo = (hn - h_hi.astype(F32)).astype(BF16)
    w_hi = w.astype(BF16)
    w_lo = (w - w_hi.astype(F32)).astype(BF16)
    logits = (jnp.dot(h_hi, w_hi, preferred_element_type=F32)
              + jnp.dot(h_hi, w_lo, preferred_element_type=F32)
              + jnp.dot(h_lo, w_hi, preferred_element_type=F32))
    lane = lax.broadcasted_iota(jnp.int32, logits.shape, 1).astype(F32)
    big = float(N_EXPERTS)
    m1 = jnp.max(logits, axis=1, keepdims=True)
    e1 = jnp.min(jnp.where(logits == m1, lane, big), axis=1, keepdims=True)
    rest = jnp.where(lane == e1, -jnp.inf, logits)
    m2 = jnp.max(rest, axis=1, keepdims=True)
    e2 = jnp.min(jnp.where(rest == m2, lane, big), axis=1, keepdims=True)
    t = jnp.exp(m2 - m1)
    g1 = 1.0 / (1.0 + t)
    g2 = t / (1.0 + t)
    o_ref[...] = jnp.where(lane == 0.0, e1,
                 jnp.where(lane == 1.0, e2,
                 jnp.where(lane == 2.0, g1,
                 jnp.where(lane == 3.0, g2, 0.0))))


def router(x, g, w_router):
    n, d = x.shape
    tm = ROUTER_TILE
    return pl.pallas_call(
        _router_kernel,
        out_shape=jax.ShapeDtypeStruct((n, N_EXPERTS), F32),
        grid=(n // tm,),
        in_specs=[
            pl.BlockSpec((tm, d), lambda i: (i, 0)),
            pl.BlockSpec((1, d), lambda i: (0, 0)),
            pl.BlockSpec((d, N_EXPERTS), lambda i: (0, 0)),
        ],
        out_specs=pl.BlockSpec((tm, N_EXPERTS), lambda i: (i, 0)),
        compiler_params=_params("parallel"),
        name="router",
    )(x, g.reshape(1, d), w_router)


def _expert_kernel(tile_e_ref, tile_rows_ref, row_tok_ref, x_hbm, g_ref, wg_ref, wu_ref, wd_ref, o_ref,
                   stage_ref, xs_ref, sem, *, rows_per_step):
    t = pl.program_id(0)
    f = pl.program_id(1)
    n_t = pl.num_programs(0)
    nf = pl.num_programs(1)
    tm = xs_ref.shape[0]
    n_stage = stage_ref.shape[0]
    n_sub = tile_rows_ref[t]

    def row_copy(tile, r):
        tok = row_tok_ref[tile * tm + r]
        return pltpu.make_async_copy(x_hbm.at[pl.ds(tok, 1), :], stage_ref.at[pl.ds(r, 1), :], sem)

    def wait_stage():
        pltpu.make_async_copy(x_hbm.at[pl.ds(0, n_stage), :], stage_ref, sem).wait()

    @pl.when(f == 0)
    def _():
        @pl.when(t == 0)
        def _():
            def issue(r, carry):
                row_copy(0, r).start()
                return carry
            lax.fori_loop(0, n_stage, issue, 0)
        wait_stage()
        xs_ref[...] = _rms_normalize(stage_ref[:tm, :], g_ref[...]).astype(BF16)
        o_ref[...] = jnp.zeros(o_ref.shape, o_ref.dtype)

    nxt = jnp.minimum(t + 1, n_t - 1)

    def prefetch_rows():
        for u in range(rows_per_step):
            row_copy(nxt, f * rows_per_step + u).start(priority=1)

    def swiglu(rows):
        xs = xs_ref[rows, :]
        gate = jnp.dot(xs, wg_ref[...].astype(BF16), preferred_element_type=F32)
        up = jnp.dot(xs, wu_ref[...].astype(BF16), preferred_element_type=F32)
        act = (_silu(gate) * up).astype(BF16)
        o_ref[rows, :] += jnp.dot(act, wd_ref[...].astype(BF16), preferred_element_type=F32)

    for occupied in range(MOE_TILE_SUBS + 1):
        @pl.when(n_sub == occupied)
        def _(occupied=occupied):
            prefetch_rows()
            if occupied:
                swiglu(slice(0, occupied * MOE_SUB_TILE))

    @pl.when((t == n_t - 1) & (f == nf - 1))
    def _():
        wait_stage()


def expert_ffn(x, g, wg, wu, wd, tile_e, tile_rows, row_tok, n_rows):
    n, d = x.shape
    fe = wg.shape[2]
    tm, tf = MOE_TILE_SUBS * MOE_SUB_TILE, MOE_F_TILE
    nf = fe // tf
    rows_per_step = -(-tm // nf)
    n_stage = rows_per_step * nf
    row_tok = jnp.concatenate([row_tok, jnp.zeros((n_stage - tm,), jnp.int32)])

    def f_idx(t, f, tr):
        return jnp.where(tr[t] > 0, f, nf - 1)

    return pl.pallas_call(
        functools.partial(_expert_kernel, rows_per_step=rows_per_step),
        out_shape=jax.ShapeDtypeStruct((n_rows, d), F32),
        grid_spec=pltpu.PrefetchScalarGridSpec(
            num_scalar_prefetch=3,
            grid=(n_rows // tm, nf),
            in_specs=[
                pl.BlockSpec(memory_space=pl.ANY),
                pl.BlockSpec((1, d), lambda t, f, te, tr, rt: (0, 0)),
                pl.BlockSpec((None, d, tf), lambda t, f, te, tr, rt: (te[t], 0, f_idx(t, f, tr))),
                pl.BlockSpec((None, d, tf), lambda t, f, te, tr, rt: (te[t], 0, f_idx(t, f, tr))),
                pl.BlockSpec((None, tf, d), lambda t, f, te, tr, rt: (te[t], f_idx(t, f, tr), 0)),
            ],
            out_specs=pl.BlockSpec((tm, d), lambda t, f, te, tr, rt: (t, 0)),
            scratch_shapes=[
                pltpu.VMEM((n_stage, d), F32),
                pltpu.VMEM((tm, d), BF16),
                pltpu.SemaphoreType.DMA,
            ],
        ),
        compiler_params=_params("arbitrary", "arbitrary"),
        name="moe_experts",
    )(tile_e, tile_rows, row_tok, x, g.reshape(1, d), wg, wu, wd)


def _combine_kernel(dest_ref, x_ref, gates_ref, g_ref, ys_hbm, o_ref, buf_ref, sems):
    t = pl.program_id(0)
    n_t = pl.num_programs(0)
    tm = x_ref.shape[0]
    unroll = 4

    def start_tile(tile, slot):
        def issue(i, carry):
            for u in range(unroll):
                r = i * unroll + u
                for k in range(TOP_K):
                    row = dest_ref[(tile * tm + r) * TOP_K + k]
                    pltpu.make_async_copy(ys_hbm.at[pl.ds(row, 1), :], buf_ref.at[slot, k, pl.ds(r, 1), :],
                                          sems.at[slot]).start(priority=k)
            return carry
        lax.fori_loop(0, tm // unroll, issue, 0)

    slot = lax.rem(t, 2)

    @pl.when(t == 0)
    def _():
        start_tile(0, 0)

    @pl.when(t + 1 < n_t)
    def _():
        start_tile(t + 1, 1 - slot)

    pltpu.make_async_copy(buf_ref.at[slot], buf_ref.at[slot], sems.at[slot]).wait()
    gates = gates_ref[...]
    y = x_ref[...] + gates[:, 2:3] * buf_ref[slot, 0] + gates[:, 3:4] * buf_ref[slot, 1]
    o_ref[...] = _rms_normalize(y, g_ref[...])


def combine(x, gates, final_g, ys, dest):
    n, d = x.shape
    tm = COMBINE_TILE
    return pl.pallas_call(
        _combine_kernel,
        out_shape=jax.ShapeDtypeStruct((n, d), F32),
        grid_spec=pltpu.PrefetchScalarGridSpec(
            num_scalar_prefetch=1,
            grid=(n // tm,),
            in_specs=[
                pl.BlockSpec((tm, d), lambda t, dst: (t, 0)),
                pl.BlockSpec((tm, N_EXPERTS), lambda t, dst: (t, 0)),
                pl.BlockSpec((1, d), lambda t, dst: (0, 0)),
                pl.BlockSpec(memory_space=pl.ANY),
            ],
            out_specs=pl.BlockSpec((tm, d), lambda t, dst: (t, 0)),
            scratch_shapes=[pltpu.VMEM((2, TOP_K, tm, d), F32), pltpu.SemaphoreType.DMA((2,))],
        ),
        compiler_params=_params("arbitrary"),
        name="moe_combine",
    )(dest, x, gates, final_g.reshape(1, d), ys)


def _routing_tables(gates_out, n_tok):
    sub, slots = MOE_SUB_TILE, MOE_TILE_SUBS
    tm = slots * sub
    n_asg = n_tok * TOP_K
    max_subs = n_asg // sub + N_EXPERTS
    n_tiles = (max_subs + N_EXPERTS * (slots - 1)) // slots
    n_rows = n_tiles * tm
    i32 = jnp.int32
    flat_e = gates_out[:, :TOP_K].astype(i32).reshape(-1)
    onehot = (flat_e[:, None] == jnp.arange(N_EXPERTS, dtype=i32)[None, :]).astype(i32)
    csum = jnp.cumsum(onehot, axis=0)
    rank = jnp.sum(csum * onehot, axis=1) - 1
    counts = csum[-1]
    subs = (counts + sub - 1) // sub
    tiles = (subs + slots - 1) // slots
    base = subs // jnp.maximum(tiles, 1)
    extra = subs - base * tiles
    tile_end = jnp.cumsum(tiles)
    tile_start = tile_end - tiles

    pick = lambda per_expert: jnp.sum(per_expert[None, :] * onehot, axis=1)
    a_base, a_extra, a_start = pick(base), pick(extra), pick(tile_start)
    q = rank // sub
    big = a_extra * (a_base + 1)
    small_base = jnp.maximum(a_base, 1)
    k = jnp.where(q < big, q // (a_base + 1), a_extra + (q - big) // small_base)
    slot = jnp.where(q < big, q % (a_base + 1), (q - big) % small_base)
    dest = ((a_start + k) * tm + slot * sub + rank % sub).astype(i32)
    row_tok = jnp.zeros((n_rows,), i32).at[dest].set(jnp.arange(n_asg, dtype=i32) // TOP_K)

    n_used = tile_end[-1]
    tile_id = jnp.minimum(jnp.arange(n_tiles, dtype=i32), n_used - 1)
    tile_e = jnp.minimum(jnp.sum((tile_end[None, :] <= tile_id[:, None]).astype(i32), axis=1),
                         N_EXPERTS - 1).astype(i32)
    k_tile = tile_id - tile_start[tile_e]
    occupied = base[tile_e] + (k_tile < extra[tile_e]).astype(i32)
    tile_rows = jnp.where(jnp.arange(n_tiles, dtype=i32) < n_used, occupied, 0).astype(i32)
    return dest, row_tok, tile_e, tile_rows, n_rows


def kernel(x, attn_a_norm_g, w_qkv_a, w_o_a, kv_norm_g, w_kv, attn_b_norm_g, w_q_b,
           lam_q1, lam_k1, lam_q2, lam_k2, subln_g, w_o_b, ffn_norm_g, w_gate, w_up,
           w_down, w_router, w_exp_gate, w_exp_up, w_exp_down, final_norm_g):
    bsz, seq, d = x.shape
    n = bsz * seq
    n_heads_a = d // HEAD_DIM
    n_heads_b = d // (2 * HEAD_DIM)
    assert ffn_norm_g.shape[0] == 2 and attn_a_norm_g.shape[0] == 1 and attn_b_norm_g.shape[0] == 1
    bf = lambda w: w.astype(BF16)

    xf = x.reshape(n, d)

    qkv = norm_matmul(xf, attn_a_norm_g[0], w_qkv_a[0], HEAD_DIM)
    o = moba_attention(qkv, bsz, seq, n_heads_a)
    xf = grouped_matmul_residual(o, bf(w_o_a[0]), xf)
    act = norm_swiglu_up(xf, ffn_norm_g[0], w_gate[0], w_up[0])
    xf = matmul_residual(act, bf(w_down[0]), xf)

    kv = norm_matmul(xf, kv_norm_g, w_kv, 2 * HEAD_DIM)
    q = norm_matmul(xf, attn_b_norm_g[0], bf(w_q_b[0]), 2 * HEAD_DIM)
    lam_init = 0.8 - 0.6 * math.exp(-0.3 * 1)
    o = diff_attention(q, kv, lam_q1[0], lam_k1[0], lam_q2[0], lam_k2[0],
                       subln_g[0], lam_init, bsz, seq, n_heads_b)
    xf = grouped_matmul_residual(o, bf(w_o_b[0]), xf)
    gates_out = router(xf, ffn_norm_g[1], w_router[0])
    dest, row_tok, tile_e, tile_rows, n_rows = _routing_tables(gates_out, n)
    ys = expert_ffn(xf, ffn_norm_g[1], w_exp_gate[0], w_exp_up[0], w_exp_down[0],
                    tile_e, tile_rows, row_tok, n_rows)
    out = combine(xf, gates_out, final_norm_g, ys, dest)
    return out.reshape(bsz, seq, d)
```

```python
import functools
import math

import jax
import jax.numpy as jnp
from jax import lax
from jax.experimental import pallas as pl
from jax.experimental.pallas import tpu as pltpu

F32 = jnp.float32
BF16 = jnp.bfloat16

HEAD_DIM = 128
MOBA_BLOCK = 256
MOBA_TOPK = 3
N_EXPERTS = 8
TOP_K = 2
RMS_EPS = 1e-6
NEG_INF = -1e30

VMEM_LIMIT_BYTES = 56 * 1024 * 1024

ROW_TILE = 1024
COL_TILE = 1024
FFN_COL_TILE = 512
ATT_BLOCK = 256
MOE_SUB_TILE = 256
MOE_TILE_SUBS = 5
MOE_F_TILE = 512
ROUTER_TILE = 512
COMBINE_TILE = 256

_NT = (((1,), (1,)), ((), ()))


def _params(*sem):
    return pltpu.CompilerParams(dimension_semantics=sem, vmem_limit_bytes=VMEM_LIMIT_BYTES)


def _rms_normalize(x, g):
    ms = jnp.mean(x * x, axis=-1, keepdims=True)
    return (x * lax.rsqrt(ms + RMS_EPS)) * g


def _norm_matmul_kernel(x_ref, g_ref, w_ref, o_ref, xn_ref):
    @pl.when(pl.program_id(1) == 0)
    def _():
        xn_ref[...] = _rms_normalize(x_ref[...], g_ref[...]).astype(BF16)

    w = w_ref[...].astype(BF16)
    res = jnp.dot(xn_ref[...], w, preferred_element_type=F32).astype(o_ref.dtype)
    n_groups, _, width = o_ref.shape
    for c in range(n_groups):
        o_ref[c] = res[:, c * width:(c + 1) * width]


def norm_matmul(x, g, w, width):
    n, d = x.shape
    cols = w.shape[1]
    tm, tn = ROW_TILE, min(COL_TILE, cols)
    return pl.pallas_call(
        _norm_matmul_kernel,
        out_shape=jax.ShapeDtypeStruct((cols // width, n, width), BF16),
        grid=(n // tm, cols // tn),
        in_specs=[
            pl.BlockSpec((tm, d), lambda i, j: (i, 0)),
            pl.BlockSpec((1, d), lambda i, j: (0, 0)),
            pl.BlockSpec((d, tn), lambda i, j: (0, j)),
        ],
        out_specs=pl.BlockSpec((tn // width, tm, width), lambda i, j: (j, i, 0)),
        scratch_shapes=[pltpu.VMEM((tm, d), BF16)],
        compiler_params=_params("parallel", "arbitrary"),
        name="norm_matmul",
    )(x, g.reshape(1, d), w)


def _silu(g):
    return g * (1.0 / (1.0 + jnp.exp(-g)))


def _norm_swiglu_up_kernel(x_ref, g_ref, wg_ref, wu_ref, o_ref, xn_ref):
    @pl.when(pl.program_id(1) == 0)
    def _():
        xn_ref[...] = _rms_normalize(x_ref[...], g_ref[...]).astype(BF16)

    xn = xn_ref[...]
    gate = jnp.dot(xn, wg_ref[...].astype(BF16), preferred_element_type=F32)
    up = jnp.dot(xn, wu_ref[...].astype(BF16), preferred_element_type=F32)
    o_ref[...] = (_silu(gate) * up).astype(o_ref.dtype)


def norm_swiglu_up(x, g, wg, wu):
    n, d = x.shape
    f = wg.shape[1]
    tm, tf = ROW_TILE, FFN_COL_TILE
    return pl.pallas_call(
        _norm_swiglu_up_kernel,
        out_shape=jax.ShapeDtypeStruct((n, f), BF16),
        grid=(n // tm, f // tf),
        in_specs=[
            pl.BlockSpec((tm, d), lambda i, j: (i, 0)),
            pl.BlockSpec((1, d), lambda i, j: (0, 0)),
            pl.BlockSpec((d, tf), lambda i, j: (0, j)),
            pl.BlockSpec((d, tf), lambda i, j: (0, j)),
        ],
        out_specs=pl.BlockSpec((tm, tf), lambda i, j: (i, j)),
        scratch_shapes=[pltpu.VMEM((tm, d), BF16)],
        compiler_params=_params("parallel", "arbitrary"),
        name="norm_swiglu_up",
    )(x, g.reshape(1, d), wg, wu)


def _matmul_residual_kernel(a_ref, w_ref, r_ref, o_ref):
    o_ref[...] = r_ref[...] + jnp.dot(a_ref[...], w_ref[...], preferred_element_type=F32)


def _grouped_matmul_residual_kernel(a_ref, w_ref, r_ref, o_ref):
    a = jnp.concatenate([a_ref[c] for c in range(a_ref.shape[0])], axis=1)
    o_ref[...] = r_ref[...] + jnp.dot(a, w_ref[...], preferred_element_type=F32)


def grouped_matmul_residual(a, w, res):
    groups, n, width = a.shape
    k = groups * width
    cols = w.shape[1]
    tm, tn = ROW_TILE, min(COL_TILE, cols)
    return pl.pallas_call(
        _grouped_matmul_residual_kernel,
        out_shape=jax.ShapeDtypeStruct((n, cols), F32),
        grid=(n // tm, cols // tn),
        in_specs=[
            pl.BlockSpec((groups, tm, width), lambda i, j: (0, i, 0)),
            pl.BlockSpec((k, tn), lambda i, j: (0, j)),
            pl.BlockSpec((tm, tn), lambda i, j: (i, j)),
        ],
        out_specs=pl.BlockSpec((tm, tn), lambda i, j: (i, j)),
        compiler_params=_params("parallel", "arbitrary"),
        name="grouped_matmul_residual",
    )(a, w, res)


def matmul_residual(a, w, res):
    n, k = a.shape
    cols = w.shape[1]
    tm, tn = ROW_TILE, min(COL_TILE, cols)
    while 2 * (2 * k * (tm + tn) + 8 * tm * tn) + 4 * tm * tn > VMEM_LIMIT_BYTES:
        tn //= 2
    return pl.pallas_call(
        _matmul_residual_kernel,
        out_shape=jax.ShapeDtypeStruct((n, cols), F32),
        grid=(n // tm, cols // tn),
        in_specs=[
            pl.BlockSpec((tm, k), lambda i, j: (i, 0)),
            pl.BlockSpec((k, tn), lambda i, j: (0, j)),
            pl.BlockSpec((tm, tn), lambda i, j: (i, j)),
        ],
        out_specs=pl.BlockSpec((tm, tn), lambda i, j: (i, j)),
        compiler_params=_params("parallel", "arbitrary"),
        name="matmul_residual",
    )(a, w, res)


_N_GAMMA_PARTS = 3
_OFF_LANE = 8
_BLK_LANE = _OFF_LANE + _N_GAMMA_PARTS


def _key_aug_table(seq):
    pos = jnp.arange(seq, dtype=jnp.int32)[:, None]
    lane = jnp.arange(HEAD_DIM, dtype=jnp.int32)[None, :]
    off = (pos % ATT_BLOCK).astype(F32)
    base = (pos - pos % ATT_BLOCK).astype(F32)
    tab = jnp.where(lane == pos // ATT_BLOCK, 1.0, 0.0)
    tab = jnp.where((lane >= _OFF_LANE) & (lane < _BLK_LANE), off, tab)
    tab = jnp.where((lane >= _BLK_LANE) & (lane < _BLK_LANE + _N_GAMMA_PARTS), base, tab)
    return tab.astype(BF16)


def _query_aug_rows(n_heads):
    hh = jnp.arange(1, n_heads + 1, dtype=F32)
    gamma = jnp.exp2(-8.0 * hh / n_heads) * (HEAD_DIM ** 0.5)
    parts = []
    rest = gamma
    for _ in range(_N_GAMMA_PARTS):
        part = rest.astype(BF16).astype(F32)
        parts.append(part)
        rest = rest - part
    lane = jnp.arange(HEAD_DIM, dtype=jnp.int32)[None, :]
    rows = jnp.zeros((n_heads, HEAD_DIM), F32)
    for p, part in enumerate(parts):
        rows = jnp.where((lane == _OFF_LANE + p) | (lane == _BLK_LANE + p), part[:, None], rows)
    return rows.reshape(n_heads, 1, HEAD_DIM)


_EXP2_SCALE = (HEAD_DIM ** -0.5) * math.log2(math.e)


def _causal_mask():
    blk = ATT_BLOCK
    return (lax.broadcasted_iota(jnp.int32, (blk, blk), 0)
            >= lax.broadcasted_iota(jnp.int32, (blk, blk), 1))


def _block_softmax(q_aug, kaug_ref, i, tri):
    blk = ATT_BLOCK
    z_own = lax.dot_general(q_aug, kaug_ref[i * blk:(i + 1) * blk, :], _NT, preferred_element_type=F32)
    z_own = jnp.where(tri, z_own, NEG_INF)
    m = jnp.max(z_own, axis=1, keepdims=True)
    if i == 0:
        p_own = jnp.exp2((z_own - m) * _EXP2_SCALE)
        return None, p_own, jnp.sum(p_own, axis=1, keepdims=True)
    z_past = lax.dot_general(q_aug, kaug_ref[:i * blk, :], _NT, preferred_element_type=F32)
    m = jnp.maximum(m, jnp.max(z_past, axis=1, keepdims=True))
    p_own = jnp.exp2((z_own - m) * _EXP2_SCALE)
    p_past = jnp.exp2((z_past - m) * _EXP2_SCALE)
    l = jnp.sum(p_own, axis=1, keepdims=True) + jnp.sum(p_past, axis=1, keepdims=True)
    return p_past, p_own, l


def _moba_kernel(q_ref, k_ref, v_ref, ktab_ref, qrow_ref, o_ref, kaug_ref, kmean_ref, *, n_blk):
    blk, hd = ATT_BLOCK, HEAD_DIM
    kaug_ref[:, :hd] = k_ref[...]
    kaug_ref[:, hd:] = ktab_ref[...]
    for j in range(n_blk):
        kj = k_ref[j * blk:(j + 1) * blk, :].astype(F32)
        kmean_ref[j:j + 1, :] = jnp.mean(kj, axis=0, keepdims=True)

    km = kmean_ref[...]
    km_hi = km.astype(BF16)
    km_lo = (km - km_hi.astype(F32)).astype(BF16)
    q_all = q_ref[...]
    gate_t = (lax.dot_general(km_hi, q_all, _NT, preferred_element_type=F32)
              + lax.dot_general(km_lo, q_all, _NT, preferred_element_type=F32))

    tri = _causal_mask()
    eye = (lax.broadcasted_iota(jnp.int32, (blk, blk), 0)
           == lax.broadcasted_iota(jnp.int32, (blk, blk), 1)).astype(BF16)
    blk_id = lax.broadcasted_iota(jnp.int32, (n_blk, blk), 0)
    q_row = qrow_ref[...]

    for i in range(n_blk):
        aug = jnp.broadcast_to(q_row, (blk, hd))
        if i > MOBA_TOPK:
            g = gate_t[:, i * blk:(i + 1) * blk]
            cnt = jnp.zeros(g.shape, F32)
            for jp in range(i):
                row = g[jp:jp + 1, :]
                beats = (row > g) | ((row == g) & (jp < blk_id))
                cnt = cnt + jnp.where(beats, 1.0, 0.0)
            drop_t = jnp.where((cnt >= float(MOBA_TOPK)) & (blk_id < i), 1.0, 0.0)
            drop_t = jnp.concatenate([drop_t, jnp.zeros((hd - n_blk, blk), F32)], axis=0)
            drop = lax.dot_general(eye, drop_t.astype(BF16), _NT, preferred_element_type=F32)
            aug = aug + drop * NEG_INF
        q_aug = jnp.concatenate([q_ref[i * blk:(i + 1) * blk, :], aug.astype(BF16)], axis=1)
        p_past, p_own, l = _block_softmax(q_aug, kaug_ref, i, tri)
        o = jnp.dot(p_own.astype(BF16), v_ref[i * blk:(i + 1) * blk, :], preferred_element_type=F32)
        if i > 0:
            o = o + jnp.dot(p_past.astype(BF16), v_ref[:i * blk, :], preferred_element_type=F32)
        o_ref[i * blk:(i + 1) * blk, :] = (o / l).astype(o_ref.dtype)


def moba_attention(qkv, bsz, seq, n_heads):
    n = bsz * seq
    blk, hd = ATT_BLOCK, HEAD_DIM
    n_blk = seq // blk
    return pl.pallas_call(
        functools.partial(_moba_kernel, n_blk=n_blk),
        out_shape=jax.ShapeDtypeStruct((n_heads, n, hd), BF16),
        grid=(bsz, n_heads),
        in_specs=[
            pl.BlockSpec((None, seq, hd), lambda b, h: (h, b, 0)),
            pl.BlockSpec((None, seq, hd), lambda b, h: (n_heads + h, b, 0)),
            pl.BlockSpec((None, seq, hd), lambda b, h: (2 * n_heads + h, b, 0)),
            pl.BlockSpec((seq, hd), lambda b, h: (0, 0)),
            pl.BlockSpec((None, 1, hd), lambda b, h: (h, 0, 0)),
        ],
        out_specs=pl.BlockSpec((None, seq, hd), lambda b, h: (h, b, 0)),
        scratch_shapes=[
            pltpu.VMEM((seq, 2 * hd), BF16),
            pltpu.VMEM((n_blk, hd), F32),
        ],
        compiler_params=_params("parallel", "parallel"),
        name="moba_attention",
    )(qkv, qkv, qkv, _key_aug_table(seq), _query_aug_rows(n_heads))


def _diff_kernel(q_ref, k_ref, v_ref, ktab_ref, qrow_ref, lq1_ref, lk1_ref, lq2_ref, lk2_ref, sg_ref,
                 o_ref, qaug_ref, kaug_ref, *, n_blk, lam_init):
    blk, hd = ATT_BLOCK, HEAD_DIM
    seq = n_blk * blk
    q_row = jnp.broadcast_to(qrow_ref[...], (seq, hd)).astype(BF16)
    for c in range(2):
        qaug_ref[c, :, :hd] = q_ref[:, c * hd:(c + 1) * hd]
        qaug_ref[c, :, hd:] = q_row
        kaug_ref[c, :, :hd] = k_ref[:, c * hd:(c + 1) * hd]
        kaug_ref[c, :, hd:] = ktab_ref[...]

    lam = (jnp.exp(jnp.sum(lq1_ref[...] * lk1_ref[...], axis=1, keepdims=True))
           - jnp.exp(jnp.sum(lq2_ref[...] * lk2_ref[...], axis=1, keepdims=True))
           + lam_init)
    tri = _causal_mask()

    for i in range(n_blk):
        rows = slice(i * blk, (i + 1) * blk)
        p1_past, p1_own, l1 = _block_softmax(qaug_ref[0, rows, :], kaug_ref.at[0], i, tri)
        p2_past, p2_own, l2 = _block_softmax(qaug_ref[1, rows, :], kaug_ref.at[1], i, tri)
        r1 = 1.0 / l1
        r2 = lam / l2
        w_own = (p1_own * r1 - p2_own * r2).astype(BF16)
        o = jnp.dot(w_own, v_ref[rows, :], preferred_element_type=F32)
        if i > 0:
            w_past = (p1_past * r1 - p2_past * r2).astype(BF16)
            o = o + jnp.dot(w_past, v_ref[:i * blk, :], preferred_element_type=F32)
        o = _rms_normalize(o, sg_ref[...]) * (1.0 - lam_init)
        o_ref[rows, :] = o.astype(o_ref.dtype)


def diff_attention(q, kv, lq1, lk1, lq2, lk2, subln_g, lam_init, bsz, seq, n_heads):
    n = bsz * seq
    blk, hd = ATT_BLOCK, HEAD_DIM
    n_blk = seq // blk
    vec = lambda a: a.reshape(1, hd)
    vec_spec = pl.BlockSpec((1, hd), lambda b, h: (0, 0))
    return pl.pallas_call(
        functools.partial(_diff_kernel, n_blk=n_blk, lam_init=lam_init),
        out_shape=jax.ShapeDtypeStruct((n_heads, n, 2 * hd), BF16),
        grid=(bsz, n_heads),
        in_specs=[
            pl.BlockSpec((None, seq, 2 * hd), lambda b, h: (h, b, 0)),
            pl.BlockSpec((None, seq, 2 * hd), lambda b, h: (h, b, 0)),
            pl.BlockSpec((None, seq, 2 * hd), lambda b, h: (n_heads + h, b, 0)),
            pl.BlockSpec((seq, hd), lambda b, h: (0, 0)),
            pl.BlockSpec((None, 1, hd), lambda b, h: (h, 0, 0)),
            vec_spec, vec_spec, vec_spec, vec_spec,
            pl.BlockSpec((1, 2 * hd), lambda b, h: (0, 0)),
        ],
        out_specs=pl.BlockSpec((None, seq, 2 * hd), lambda b, h: (h, b, 0)),
        scratch_shapes=[
            pltpu.VMEM((2, seq, 2 * hd), BF16),
            pltpu.VMEM((2, seq, 2 * hd), BF16),
        ],
        compiler_params=_params("parallel", "parallel"),
        name="diff_attention",
    )(q, kv, kv, _key_aug_table(seq), _query_aug_rows(n_heads),
      vec(lq1), vec(lk1), vec(lq2), vec(lk2), subln_g.reshape(1, 2 * hd))


def _router_kernel(x_ref, g_ref, w_ref, o_ref):
    hn = _rms_normalize(x_ref[...], g_ref[...])
    w = w_ref[...]
    h_hi = hn.astype(BF16)
    h_lo = (hn - h_hi.astype(F32)).astype(BF16)
    w_hi = w.astype(BF16)
    w_lo = (w - w_hi.astype(F32)).astype(BF16)
    logits = (jnp.dot(h_hi, w_hi, preferred_element_type=F32)
              + jnp.dot(h_hi, w_lo, preferred_element_type=F32)
              + jnp.dot(h_lo, w_hi, preferred_element_type=F32))
    lane = lax.broadcasted_iota(jnp.int32, logits.shape, 1).astype(F32)
    big = float(N_EXPERTS)
    m1 = jnp.max(logits, axis=1, keepdims=True)
    e1 = jnp.min(jnp.where(logits == m1, lane, big), axis=1, keepdims=True)
    rest = jnp.where(lane == e1, -jnp.inf, logits)
    m2 = jnp.max(rest, axis=1, keepdims=True)
    e2 = jnp.min(jnp.where(rest == m2, lane, big), axis=1, keepdims=True)
    t = jnp.exp(m2 - m1)
    g1 = 1.0 / (1.0 + t)
    g2 = t / (1.0 + t)
    o_ref[...] = jnp.where(lane == 0.0, e1,
                 jnp.where(lane == 1.0, e2,
                 jnp.where(lane == 2.0, g1,
                 jnp.where(lane == 3.0, g2, 0.0))))


def router(x, g, w_router):
    n, d = x.shape
    tm = ROUTER_TILE
    return pl.pallas_call(
        _router_kernel,
        out_shape=jax.ShapeDtypeStruct((n, N_EXPERTS), F32),
        grid=(n // tm,),
        in_specs=[
            pl.BlockSpec((tm, d), lambda i: (i, 0)),
            pl.BlockSpec((1, d), lambda i: (0, 0)),
            pl.BlockSpec((d, N_EXPERTS), lambda i: (0, 0)),
        ],
        out_specs=pl.BlockSpec((tm, N_EXPERTS), lambda i: (i, 0)),
        compiler_params=_params("parallel"),
        name="router",
    )(x, g.reshape(1, d), w_router)


def _dispatch_kernel(row_tok_ref, sub_valid_ref, x_hbm, g_ref, o_ref, buf_ref, sems):
    s = pl.program_id(0)
    n_s = pl.num_programs(0)
    tm = o_ref.shape[0]
    unroll = 8

    def start_sub(sub, slot):
        def issue(i, carry):
            for u in range(unroll):
                r = i * unroll + u
                tok = row_tok_ref[sub * tm + r]
                pltpu.make_async_copy(x_hbm.at[pl.ds(tok, 1), :], buf_ref.at[slot, pl.ds(r, 1), :],
                                      sems.at[slot]).start(priority=u % 2)
            return carry
        lax.fori_loop(0, tm // unroll, issue, 0)

    slot = lax.rem(s, 2)
    valid = sub_valid_ref[s] > 0

    @pl.when((s == 0) & valid)
    def _():
        start_sub(0, 0)

    nxt = jnp.minimum(s + 1, n_s - 1)

    @pl.when((s + 1 < n_s) & (sub_valid_ref[nxt] > 0))
    def _():
        start_sub(s + 1, 1 - slot)

    @pl.when(valid)
    def _():
        pltpu.make_async_copy(buf_ref.at[slot], buf_ref.at[slot], sems.at[slot]).wait()
        o_ref[...] = _rms_normalize(buf_ref[slot], g_ref[...]).astype(o_ref.dtype)

    @pl.when(jnp.logical_not(valid))
    def _():
        o_ref[...] = jnp.zeros(o_ref.shape, o_ref.dtype)


def dispatch(x, g, row_tok, sub_valid, n_rows):
    n, d = x.shape
    tm = MOE_SUB_TILE
    return pl.pallas_call(
        _dispatch_kernel,
        out_shape=jax.ShapeDtypeStruct((n_rows, d), BF16),
        grid_spec=pltpu.PrefetchScalarGridSpec(
            num_scalar_prefetch=2,
            grid=(n_rows // tm,),
            in_specs=[
                pl.BlockSpec(memory_space=pl.ANY),
                pl.BlockSpec((1, d), lambda s, rt, sv: (0, 0)),
            ],
            out_specs=pl.BlockSpec((tm, d), lambda s, rt, sv: (s, 0)),
            scratch_shapes=[pltpu.VMEM((2, tm, d), F32), pltpu.SemaphoreType.DMA((2,))],
        ),
        compiler_params=_params("arbitrary"),
        name="moe_dispatch",
    )(row_tok, sub_valid, x, g.reshape(1, d))


def _expert_kernel(tile_e_ref, tile_rows_ref, xs_ref, wg_ref, wu_ref, wd_ref, o_ref):
    t = pl.program_id(0)
    f = pl.program_id(1)
    n_sub = tile_rows_ref[t]

    @pl.when(f == 0)
    def _():
        o_ref[...] = jnp.zeros(o_ref.shape, o_ref.dtype)

    def swiglu(rows):
        xs = xs_ref[rows, :]
        gate = jnp.dot(xs, wg_ref[...].astype(BF16), preferred_element_type=F32)
        up = jnp.dot(xs, wu_ref[...].astype(BF16), preferred_element_type=F32)
        act = (_silu(gate) * up).astype(BF16)
        o_ref[rows, :] += jnp.dot(act, wd_ref[...].astype(BF16), preferred_element_type=F32)

    for occupied in range(1, MOE_TILE_SUBS + 1):
        @pl.when(n_sub == occupied)
        def _(occupied=occupied):
            swiglu(slice(0, occupied * MOE_SUB_TILE))


def expert_ffn(xs, wg, wu, wd, tile_e, tile_rows):
    n_rows, d = xs.shape
    fe = wg.shape[2]
    tm, tf = MOE_TILE_SUBS * MOE_SUB_TILE, MOE_F_TILE
    nf = fe // tf

    def f_idx(t, f, tr):
        return jnp.where(tr[t] > 0, f, nf - 1)

    return pl.pallas_call(
        _expert_kernel,
        out_shape=jax.ShapeDtypeStruct((n_rows, d), F32),
        grid_spec=pltpu.PrefetchScalarGridSpec(
            num_scalar_prefetch=2,
            grid=(n_rows // tm, nf),
            in_specs=[
                pl.BlockSpec((tm, d), lambda t, f, te, tr: (t, 0)),
                pl.BlockSpec((None, d, tf), lambda t, f, te, tr: (te[t], 0, f_idx(t, f, tr))),
                pl.BlockSpec((None, d, tf), lambda t, f, te, tr: (te[t], 0, f_idx(t, f, tr))),
                pl.BlockSpec((None, tf, d), lambda t, f, te, tr: (te[t], f_idx(t, f, tr), 0)),
            ],
            out_specs=pl.BlockSpec((tm, d), lambda t, f, te, tr: (t, 0), pipeline_mode=pl.Buffered(1)),
        ),
        compiler_params=_params("arbitrary", "arbitrary"),
        name="moe_experts",
    )(tile_e, tile_rows, xs, wg, wu, wd)


def _combine_kernel(dest_ref, x_ref, gates_ref, g_ref, ys_hbm, o_ref, buf_ref, sems):
    t = pl.program_id(0)
    n_t = pl.num_programs(0)
    tm = x_ref.shape[0]
    unroll = 4

    def start_tile(tile, slot):
        def issue(i, carry):
            for u in range(unroll):
                r = i * unroll + u
                for k in range(TOP_K):
                    row = dest_ref[(tile * tm + r) * TOP_K + k]
                    pltpu.make_async_copy(ys_hbm.at[pl.ds(row, 1), :], buf_ref.at[slot, k, pl.ds(r, 1), :],
                                          sems.at[slot]).start(priority=k)
            return carry
        lax.fori_loop(0, tm // unroll, issue, 0)

    slot = lax.rem(t, 2)

    @pl.when(t == 0)
    def _():
        start_tile(0, 0)

    @pl.when(t + 1 < n_t)
    def _():
        start_tile(t + 1, 1 - slot)

    pltpu.make_async_copy(buf_ref.at[slot], buf_ref.at[slot], sems.at[slot]).wait()
    gates = gates_ref[...]
    y = x_ref[...] + gates[:, 2:3] * buf_ref[slot, 0] + gates[:, 3:4] * buf_ref[slot, 1]
    o_ref[...] = _rms_normalize(y, g_ref[...])


def combine(x, gates, final_g, ys, dest):
    n, d = x.shape
    tm = COMBINE_TILE
    return pl.pallas_call(
        _combine_kernel,
        out_shape=jax.ShapeDtypeStruct((n, d), F32),
        grid_spec=pltpu.PrefetchScalarGridSpec(
            num_scalar_prefetch=1,
            grid=(n // tm,),
            in_specs=[
                pl.BlockSpec((tm, d), lambda t, dst: (t, 0)),
                pl.BlockSpec((tm, N_EXPERTS), lambda t, dst: (t, 0)),
                pl.BlockSpec((1, d), lambda t, dst: (0, 0)),
                pl.BlockSpec(memory_space=pl.ANY),
            ],
            out_specs=pl.BlockSpec((tm, d), lambda t, dst: (t, 0)),
            scratch_shapes=[pltpu.VMEM((2, TOP_K, tm, d), F32), pltpu.SemaphoreType.DMA((2,))],
        ),
        compiler_params=_params("arbitrary"),
        name="moe_combine",
    )(dest, x, gates, final_g.reshape(1, d), ys)


def _routing_tables(gates_out, n_tok):
    sub, slots = MOE_SUB_TILE, MOE_TILE_SUBS
    tm = slots * sub
    n_asg = n_tok * TOP_K
    max_subs = n_asg // sub + N_EXPERTS
    n_tiles = (max_subs + N_EXPERTS * (slots - 1)) // slots
    n_rows = n_tiles * tm
    i32 = jnp.int32
    flat_e = gates_out[:, :TOP_K].astype(i32).reshape(-1)
    onehot = (flat_e[:, None] == jnp.arange(N_EXPERTS, dtype=i32)[None, :]).astype(i32)
    csum = jnp.cumsum(onehot, axis=0)
    rank = jnp.sum(csum * onehot, axis=1) - 1
    counts = csum[-1]
    subs = (counts + sub - 1) // sub
    tiles = (subs + slots - 1) // slots
    base = subs // jnp.maximum(tiles, 1)
    extra = subs - base * tiles
    tile_end = jnp.cumsum(tiles)
    tile_start = tile_end - tiles

    pick = lambda per_expert: jnp.sum(per_expert[None, :] * onehot, axis=1)
    a_base, a_extra, a_start = pick(base), pick(extra), pick(tile_start)
    q = rank // sub
    big = a_extra * (a_base + 1)
    small_base = jnp.maximum(a_base, 1)
    k = jnp.where(q < big, q // (a_base + 1), a_extra + (q - big) // small_base)
    slot = jnp.where(q < big, q % (a_base + 1), (q - big) % small_base)
    dest = ((a_start + k) * tm + slot * sub + rank % sub).astype(i32)
    row_tok = jnp.zeros((n_rows,), i32).at[dest].set(jnp.arange(n_asg, dtype=i32) // TOP_K)

    n_used = tile_end[-1]
    tile_id = jnp.minimum(jnp.arange(n_tiles, dtype=i32), n_used - 1)
    tile_e = jnp.minimum(jnp.sum((tile_end[None, :] <= tile_id[:, None]).astype(i32), axis=1),
                         N_EXPERTS - 1).astype(i32)
    k_tile = tile_id - tile_start[tile_e]
    occupied = base[tile_e] + (k_tile < extra[tile_e]).astype(i32)
    tile_rows = jnp.where(jnp.arange(n_tiles, dtype=i32) < n_used, occupied, 0).astype(i32)
    sub_valid = (jnp.arange(slots, dtype=i32)[None, :] < tile_rows[:, None]).astype(i32)
    return dest, row_tok, tile_e, tile_rows, sub_valid.reshape(-1), n_rows


def kernel(x, attn_a_norm_g, w_qkv_a, w_o_a, kv_norm_g, w_kv, attn_b_norm_g, w_q_b,
           lam_q1, lam_k1, lam_q2, lam_k2, subln_g, w_o_b, ffn_norm_g, w_gate, w_up,
           w_down, w_router, w_exp_gate, w_exp_up, w_exp_down, final_norm_g):
    bsz, seq, d = x.shape
    n = bsz * seq
    n_heads_a = d // HEAD_DIM
    n_heads_b = d // (2 * HEAD_DIM)
    assert ffn_norm_g.shape[0] == 2 and attn_a_norm_g.shape[0] == 1 and attn_b_norm_g.shape[0] == 1
    bf = lambda w: w.astype(BF16)

    xf = x.reshape(n, d)

    qkv = norm_matmul(xf, attn_a_norm_g[0], w_qkv_a[0], HEAD_DIM)
    o = moba_attention(qkv, bsz, seq, n_heads_a)
    xf = grouped_matmul_residual(o, bf(w_o_a[0]), xf)
    act = norm_swiglu_up(xf, ffn_norm_g[0], w_gate[0], w_up[0])
    xf = matmul_residual(act, bf(w_down[0]), xf)

    kv = norm_matmul(xf, kv_norm_g, w_kv, 2 * HEAD_DIM)
    q = norm_matmul(xf, attn_b_norm_g[0], bf(w_q_b[0]), 2 * HEAD_DIM)
    lam_init = 0.8 - 0.6 * math.exp(-0.3 * 1)
    o = diff_attention(q, kv, lam_q1[0], lam_k1[0], lam_q2[0], lam_k2[0],
                       subln_g[0], lam_init, bsz, seq, n_heads_b)
    xf = grouped_matmul_residual(o, bf(w_o_b[0]), xf)
    gates_out = router(xf, ffn_norm_g[1], w_router[0])
    dest, row_tok, tile_e, tile_rows, sub_valid, n_rows = _routing_tables(gates_out, n)
    xs = dispatch(xf, ffn_norm_g[1], row_tok, sub_valid, n_rows)
    ys = expert_ffn(xs, w_exp_gate[0], w_exp_up[0], w_exp_down[0], tile_e, tile_rows)
    out = combine(xf, gates_out, final_norm_g, ys, dest)
    return out.reshape(bsz, seq, d)
```

```python
import functools
import math

import jax
import jax.numpy as jnp
from jax import lax
from jax.experimental import pallas as pl
from jax.experimental.pallas import tpu as pltpu

F32 = jnp.float32
BF16 = jnp.bfloat16

HEAD_DIM = 128
MOBA_BLOCK = 256
MOBA_TOPK = 3
N_EXPERTS = 8
TOP_K = 2
RMS_EPS = 1e-6
NEG_INF = -1e30

VMEM_LIMIT_BYTES = 56 * 1024 * 1024

ROW_TILE = 1024
COL_TILE = 1024
FFN_COL_TILE = 512
ATT_BLOCK = 256
MOBA_HEADS_PER_STEP = 4
DIFF_HEADS_PER_STEP = 2
MOE_SUB_TILE = 256
MOE_TILE_SUBS = 5
MOE_F_TILE = 512
ROUTER_TILE = 512
COMBINE_TILE = 256

_NT = (((1,), (1,)), ((), ()))


def _params(*sem):
    return pltpu.CompilerParams(dimension_semantics=sem, vmem_limit_bytes=VMEM_LIMIT_BYTES)


def _rms_normalize(x, g):
    ms = jnp.mean(x * x, axis=-1, keepdims=True)
    return (x * lax.rsqrt(ms + RMS_EPS)) * g


def _norm_matmul_kernel(x_ref, g_ref, w_ref, o_ref, xn_ref):
    @pl.when(pl.program_id(1) == 0)
    def _():
        xn_ref[...] = _rms_normalize(x_ref[...], g_ref[...]).astype(BF16)

    w = w_ref[...].astype(BF16)
    res = jnp.dot(xn_ref[...], w, preferred_element_type=F32).astype(o_ref.dtype)
    n_groups, _, width = o_ref.shape
    for c in range(n_groups):
        o_ref[c] = res[:, c * width:(c + 1) * width]


def norm_matmul(x, g, w, width):
    n, d = x.shape
    cols = w.shape[1]
    tm, tn = ROW_TILE, min(COL_TILE, cols)
    return pl.pallas_call(
        _norm_matmul_kernel,
        out_shape=jax.ShapeDtypeStruct((cols // width, n, width), BF16),
        grid=(n // tm, cols // tn),
        in_specs=[
            pl.BlockSpec((tm, d), lambda i, j: (i, 0)),
            pl.BlockSpec((1, d), lambda i, j: (0, 0)),
            pl.BlockSpec((d, tn), lambda i, j: (0, j)),
        ],
        out_specs=pl.BlockSpec((tn // width, tm, width), lambda i, j: (j, i, 0)),
        scratch_shapes=[pltpu.VMEM((tm, d), BF16)],
        compiler_params=_params("parallel", "arbitrary"),
        name="norm_matmul",
    )(x, g.reshape(1, d), w)


def _silu(g):
    return g * (1.0 / (1.0 + jnp.exp(-g)))


def _norm_swiglu_up_kernel(x_ref, g_ref, wg_ref, wu_ref, o_ref, xn_ref):
    @pl.when(pl.program_id(1) == 0)
    def _():
        xn_ref[...] = _rms_normalize(x_ref[...], g_ref[...]).astype(BF16)

    xn = xn_ref[...]
    gate = jnp.dot(xn, wg_ref[...].astype(BF16), preferred_element_type=F32)
    up = jnp.dot(xn, wu_ref[...].astype(BF16), preferred_element_type=F32)
    o_ref[...] = (_silu(gate) * up).astype(o_ref.dtype)


def norm_swiglu_up(x, g, wg, wu):
    n, d = x.shape
    f = wg.shape[1]
    tm, tf = ROW_TILE, FFN_COL_TILE
    return pl.pallas_call(
        _norm_swiglu_up_kernel,
        out_shape=jax.ShapeDtypeStruct((n, f), BF16),
        grid=(n // tm, f // tf),
        in_specs=[
            pl.BlockSpec((tm, d), lambda i, j: (i, 0)),
            pl.BlockSpec((1, d), lambda i, j: (0, 0)),
            pl.BlockSpec((d, tf), lambda i, j: (0, j)),
            pl.BlockSpec((d, tf), lambda i, j: (0, j)),
        ],
        out_specs=pl.BlockSpec((tm, tf), lambda i, j: (i, j)),
        scratch_shapes=[pltpu.VMEM((tm, d), BF16)],
        compiler_params=_params("parallel", "arbitrary"),
        name="norm_swiglu_up",
    )(x, g.reshape(1, d), wg, wu)


def _matmul_residual_kernel(a_ref, w_ref, r_ref, o_ref):
    o_ref[...] = r_ref[...] + jnp.dot(a_ref[...], w_ref[...], preferred_element_type=F32)


def _grouped_matmul_residual_kernel(a_ref, w_ref, r_ref, o_ref):
    a = jnp.concatenate([a_ref[c] for c in range(a_ref.shape[0])], axis=1)
    o_ref[...] = r_ref[...] + jnp.dot(a, w_ref[...], preferred_element_type=F32)


def grouped_matmul_residual(a, w, res):
    groups, n, width = a.shape
    k = groups * width
    cols = w.shape[1]
    tm, tn = ROW_TILE, min(COL_TILE, cols)
    return pl.pallas_call(
        _grouped_matmul_residual_kernel,
        out_shape=jax.ShapeDtypeStruct((n, cols), F32),
        grid=(n // tm, cols // tn),
        in_specs=[
            pl.BlockSpec((groups, tm, width), lambda i, j: (0, i, 0)),
            pl.BlockSpec((k, tn), lambda i, j: (0, j)),
            pl.BlockSpec((tm, tn), lambda i, j: (i, j)),
        ],
        out_specs=pl.BlockSpec((tm, tn), lambda i, j: (i, j)),
        compiler_params=_params("parallel", "arbitrary"),
        name="grouped_matmul_residual",
    )(a, w, res)


def matmul_residual(a, w, res):
    n, k = a.shape
    cols = w.shape[1]
    tm, tn = ROW_TILE, min(COL_TILE, cols)
    while 2 * (2 * k * (tm + tn) + 8 * tm * tn) + 4 * tm * tn > VMEM_LIMIT_BYTES:
        tn //= 2
    return pl.pallas_call(
        _matmul_residual_kernel,
        out_shape=jax.ShapeDtypeStruct((n, cols), F32),
        grid=(n // tm, cols // tn),
        in_specs=[
            pl.BlockSpec((tm, k), lambda i, j: (i, 0)),
            pl.BlockSpec((k, tn), lambda i, j: (0, j)),
            pl.BlockSpec((tm, tn), lambda i, j: (i, j)),
        ],
        out_specs=pl.BlockSpec((tm, tn), lambda i, j: (i, j)),
        compiler_params=_params("parallel", "arbitrary"),
        name="matmul_residual",
    )(a, w, res)


_N_GAMMA_PARTS = 3
_OFF_LANE = 8
_BLK_LANE = _OFF_LANE + _N_GAMMA_PARTS


def _key_aug_table(seq):
    pos = jnp.arange(seq, dtype=jnp.int32)[:, None]
    lane = jnp.arange(HEAD_DIM, dtype=jnp.int32)[None, :]
    off = (pos % ATT_BLOCK).astype(F32)
    base = (pos - pos % ATT_BLOCK).astype(F32)
    tab = jnp.where(lane == pos // ATT_BLOCK, 1.0, 0.0)
    tab = jnp.where((lane >= _OFF_LANE) & (lane < _BLK_LANE), off, tab)
    tab = jnp.where((lane >= _BLK_LANE) & (lane < _BLK_LANE + _N_GAMMA_PARTS), base, tab)
    return tab.astype(BF16)


def _query_aug_rows(n_heads):
    hh = jnp.arange(1, n_heads + 1, dtype=F32)
    gamma = jnp.exp2(-8.0 * hh / n_heads) * (HEAD_DIM ** 0.5)
    parts = []
    rest = gamma
    for _ in range(_N_GAMMA_PARTS):
        part = rest.astype(BF16).astype(F32)
        parts.append(part)
        rest = rest - part
    lane = jnp.arange(HEAD_DIM, dtype=jnp.int32)[None, :]
    rows = jnp.zeros((n_heads, HEAD_DIM), F32)
    for p, part in enumerate(parts):
        rows = jnp.where((lane == _OFF_LANE + p) | (lane == _BLK_LANE + p), part[:, None], rows)
    return rows.reshape(n_heads, 1, HEAD_DIM)


_EXP2_SCALE = (HEAD_DIM ** -0.5) * math.log2(math.e)


def _causal_mask():
    blk = ATT_BLOCK
    return (lax.broadcasted_iota(jnp.int32, (blk, blk), 0)
            >= lax.broadcasted_iota(jnp.int32, (blk, blk), 1))


def _block_softmax(q_aug, kaug_ref, i, tri):
    blk = ATT_BLOCK
    z_own = lax.dot_general(q_aug, kaug_ref[i * blk:(i + 1) * blk, :], _NT, preferred_element_type=F32)
    z_own = jnp.where(tri, z_own, NEG_INF)
    m = jnp.max(z_own, axis=1, keepdims=True)
    if i == 0:
        p_own = jnp.exp2((z_own - m) * _EXP2_SCALE)
        return None, p_own, jnp.sum(p_own, axis=1, keepdims=True)
    z_past = lax.dot_general(q_aug, kaug_ref[:i * blk, :], _NT, preferred_element_type=F32)
    m = jnp.maximum(m, jnp.max(z_past, axis=1, keepdims=True))
    p_own = jnp.exp2((z_own - m) * _EXP2_SCALE)
    p_past = jnp.exp2((z_past - m) * _EXP2_SCALE)
    l = jnp.sum(p_own, axis=1, keepdims=True) + jnp.sum(p_past, axis=1, keepdims=True)
    return p_past, p_own, l


def _moba_kernel(q_ref, k_ref, v_ref, ktab_ref, qrow_ref, o_ref, kaug_ref, kmean_ref, *, n_blk):
    for g in range(q_ref.shape[0]):
        _moba_head(q_ref.at[g], k_ref.at[g], v_ref.at[g], ktab_ref, qrow_ref.at[g], o_ref.at[g],
                   kaug_ref.at[g], kmean_ref.at[g], n_blk)


def _moba_head(q_ref, k_ref, v_ref, ktab_ref, qrow_ref, o_ref, kaug_ref, kmean_ref, n_blk):
    blk, hd = ATT_BLOCK, HEAD_DIM
    kaug_ref[:, :hd] = k_ref[...]
    kaug_ref[:, hd:] = ktab_ref[...]
    for j in range(n_blk):
        kj = k_ref[j * blk:(j + 1) * blk, :].astype(F32)
        kmean_ref[j:j + 1, :] = jnp.mean(kj, axis=0, keepdims=True)

    km = kmean_ref[...]
    km_hi = km.astype(BF16)
    km_lo = (km - km_hi.astype(F32)).astype(BF16)
    q_all = q_ref[...]
    gate_t = (lax.dot_general(km_hi, q_all, _NT, preferred_element_type=F32)
              + lax.dot_general(km_lo, q_all, _NT, preferred_element_type=F32))

    tri = _causal_mask()
    eye = (lax.broadcasted_iota(jnp.int32, (blk, blk), 0)
           == lax.broadcasted_iota(jnp.int32, (blk, blk), 1)).astype(BF16)
    blk_id = lax.broadcasted_iota(jnp.int32, (n_blk, blk), 0)
    q_row = qrow_ref[...]

    for i in range(n_blk):
        aug = jnp.broadcast_to(q_row, (blk, hd))
        if i > MOBA_TOPK:
            g = gate_t[:, i * blk:(i + 1) * blk]
            cnt = jnp.zeros(g.shape, F32)
            for jp in range(i):
                row = g[jp:jp + 1, :]
                beats = (row > g) | ((row == g) & (jp < blk_id))
                cnt = cnt + jnp.where(beats, 1.0, 0.0)
            drop_t = jnp.where((cnt >= float(MOBA_TOPK)) & (blk_id < i), 1.0, 0.0)
            drop_t = jnp.concatenate([drop_t, jnp.zeros((hd - n_blk, blk), F32)], axis=0)
            drop = lax.dot_general(eye, drop_t.astype(BF16), _NT, preferred_element_type=F32)
            aug = aug + drop * NEG_INF
        q_aug = jnp.concatenate([q_ref[i * blk:(i + 1) * blk, :], aug.astype(BF16)], axis=1)
        p_past, p_own, l = _block_softmax(q_aug, kaug_ref, i, tri)
        o = jnp.dot(p_own.astype(BF16), v_ref[i * blk:(i + 1) * blk, :], preferred_element_type=F32)
        if i > 0:
            o = o + jnp.dot(p_past.astype(BF16), v_ref[:i * blk, :], preferred_element_type=F32)
        o_ref[i * blk:(i + 1) * blk, :] = (o / l).astype(o_ref.dtype)


def moba_attention(qkv, bsz, seq, n_heads):
    n = bsz * seq
    blk, hd = ATT_BLOCK, HEAD_DIM
    n_blk = seq // blk
    hps = MOBA_HEADS_PER_STEP
    n_groups = n_heads // hps
    return pl.pallas_call(
        functools.partial(_moba_kernel, n_blk=n_blk),
        out_shape=jax.ShapeDtypeStruct((n_heads, n, hd), BF16),
        grid=(bsz, n_groups),
        in_specs=[
            pl.BlockSpec((hps, seq, hd), lambda b, h: (h, b, 0)),
            pl.BlockSpec((hps, seq, hd), lambda b, h: (n_groups + h, b, 0)),
            pl.BlockSpec((hps, seq, hd), lambda b, h: (2 * n_groups + h, b, 0)),
            pl.BlockSpec((seq, hd), lambda b, h: (0, 0)),
            pl.BlockSpec((hps, 1, hd), lambda b, h: (h, 0, 0)),
        ],
        out_specs=pl.BlockSpec((hps, seq, hd), lambda b, h: (h, b, 0)),
        scratch_shapes=[
            pltpu.VMEM((hps, seq, 2 * hd), BF16),
            pltpu.VMEM((hps, n_blk, hd), F32),
        ],
        compiler_params=_params("parallel", "parallel"),
        name="moba_attention",
    )(qkv, qkv, qkv, _key_aug_table(seq), _query_aug_rows(n_heads))


def _diff_kernel(q_ref, k_ref, v_ref, ktab_ref, qrow_ref, lq1_ref, lk1_ref, lq2_ref, lk2_ref, sg_ref,
                 o_ref, qaug_ref, kaug_ref, *, n_blk, lam_init):
    for g in range(q_ref.shape[0]):
        _diff_head(q_ref.at[g], k_ref.at[g], v_ref.at[g], ktab_ref, qrow_ref.at[g], lq1_ref, lk1_ref,
                   lq2_ref, lk2_ref, sg_ref, o_ref.at[g], qaug_ref.at[g], kaug_ref.at[g], n_blk, lam_init)


def _diff_head(q_ref, k_ref, v_ref, ktab_ref, qrow_ref, lq1_ref, lk1_ref, lq2_ref, lk2_ref, sg_ref,
               o_ref, qaug_ref, kaug_ref, n_blk, lam_init):
    blk, hd = ATT_BLOCK, HEAD_DIM
    seq = n_blk * blk
    q_row = jnp.broadcast_to(qrow_ref[...], (seq, hd)).astype(BF16)
    for c in range(2):
        qaug_ref[c, :, :hd] = q_ref[:, c * hd:(c + 1) * hd]
        qaug_ref[c, :, hd:] = q_row
        kaug_ref[c, :, :hd] = k_ref[:, c * hd:(c + 1) * hd]
        kaug_ref[c, :, hd:] = ktab_ref[...]

    lam = (jnp.exp(jnp.sum(lq1_ref[...] * lk1_ref[...], axis=1, keepdims=True))
           - jnp.exp(jnp.sum(lq2_ref[...] * lk2_ref[...], axis=1, keepdims=True))
           + lam_init)
    tri = _causal_mask()

    for i in range(n_blk):
        rows = slice(i * blk, (i + 1) * blk)
        p1_past, p1_own, l1 = _block_softmax(qaug_ref[0, rows, :], kaug_ref.at[0], i, tri)
        p2_past, p2_own, l2 = _block_softmax(qaug_ref[1, rows, :], kaug_ref.at[1], i, tri)
        r1 = 1.0 / l1
        r2 = lam / l2
        w_own = (p1_own * r1 - p2_own * r2).astype(BF16)
        o = jnp.dot(w_own, v_ref[rows, :], preferred_element_type=F32)
        if i > 0:
            w_past = (p1_past * r1 - p2_past * r2).astype(BF16)
            o = o + jnp.dot(w_past, v_ref[:i * blk, :], preferred_element_type=F32)
        o = _rms_normalize(o, sg_ref[...]) * (1.0 - lam_init)
        o_ref[rows, :] = o.astype(o_ref.dtype)


def diff_attention(q, kv, lq1, lk1, lq2, lk2, subln_g, lam_init, bsz, seq, n_heads):
    n = bsz * seq
    blk, hd = ATT_BLOCK, HEAD_DIM
    n_blk = seq // blk
    hps = DIFF_HEADS_PER_STEP
    n_groups = n_heads // hps
    vec = lambda a: a.reshape(1, hd)
    vec_spec = pl.BlockSpec((1, hd), lambda b, h: (0, 0))
    return pl.pallas_call(
        functools.partial(_diff_kernel, n_blk=n_blk, lam_init=lam_init),
        out_shape=jax.ShapeDtypeStruct((n_heads, n, 2 * hd), BF16),
        grid=(bsz, n_groups),
        in_specs=[
            pl.BlockSpec((hps, seq, 2 * hd), lambda b, h: (h, b, 0)),
            pl.BlockSpec((hps, seq, 2 * hd), lambda b, h: (h, b, 0)),
            pl.BlockSpec((hps, seq, 2 * hd), lambda b, h: (n_groups + h, b, 0)),
            pl.BlockSpec((seq, hd), lambda b, h: (0, 0)),
            pl.BlockSpec((hps, 1, hd), lambda b, h: (h, 0, 0)),
            vec_spec, vec_spec, vec_spec, vec_spec,
            pl.BlockSpec((1, 2 * hd), lambda b, h: (0, 0)),
        ],
        out_specs=pl.BlockSpec((hps, seq, 2 * hd), lambda b, h: (h, b, 0)),
        scratch_shapes=[
            pltpu.VMEM((hps, 2, seq, 2 * hd), BF16),
            pltpu.VMEM((hps, 2, seq, 2 * hd), BF16),
        ],
        compiler_params=_params("parallel", "parallel"),
        name="diff_attention",
    )(q, kv, kv, _key_aug_table(seq), _query_aug_rows(n_heads),
      vec(lq1), vec(lk1), vec(lq2), vec(lk2), subln_g.reshape(1, 2 * hd))


def _router_kernel(x_ref, g_ref, w_ref, o_ref):
    hn = _rms_normalize(x_ref[...], g_ref[...])
    w = w_ref[...]
    h_hi = hn.astype(BF16)
    h_lo = (hn - h_hi.astype(F32)).astype(BF16)
    w_hi = w.astype(BF16)
    w_lo = (w - w_hi.astype(F32)).astype(BF16)
    logits = (jnp.dot(h_hi, w_hi, preferred_element_type=F32)
              + jnp.dot(h_hi, w_lo, preferred_element_type=F32)
              + jnp.dot(h_lo, w_hi, preferred_element_type=F32))
    lane = lax.broadcasted_iota(jnp.int32, logits.shape, 1).astype(F32)
    big = float(N_EXPERTS)
    m1 = jnp.max(logits, axis=1, keepdims=True)
    e1 = jnp.min(jnp.where(logits == m1, lane, big), axis=1, keepdims=True)
    rest = jnp.where(lane == e1, -jnp.inf, logits)
    m2 = jnp.max(rest, axis=1, keepdims=True)
    e2 = jnp.min(jnp.where(rest == m2, lane, big), axis=1, keepdims=True)
    t = jnp.exp(m2 - m1)
    g1 = 1.0 / (1.0 + t)
    g2 = t / (1.0 + t)
    o_ref[...] = jnp.where(lane == 0.0, e1,
                 jnp.where(lane == 1.0, e2,
                 jnp.where(lane == 2.0, g1,
                 jnp.where(lane == 3.0, g2, 0.0))))


def router(x, g, w_router):
    n, d = x.shape
    tm = ROUTER_TILE
    return pl.pallas_call(
        _router_kernel,
        out_shape=jax.ShapeDtypeStruct((n, N_EXPERTS), F32),
        grid=(n // tm,),
        in_specs=[
            pl.BlockSpec((tm, d), lambda i: (i, 0)),
            pl.BlockSpec((1, d), lambda i: (0, 0)),
            pl.BlockSpec((d, N_EXPERTS), lambda i: (0, 0)),
        ],
        out_specs=pl.BlockSpec((tm, N_EXPERTS), lambda i: (i, 0)),
        compiler_params=_params("parallel"),
        name="router",
    )(x, g.reshape(1, d), w_router)


def _dispatch_kernel(row_tok_ref, sub_valid_ref, x_hbm, g_ref, o_ref, buf_ref, sems):
    s = pl.program_id(0)
    n_s = pl.num_programs(0)
    tm = o_ref.shape[0]
    unroll = 8

    def start_sub(sub, slot):
        def issue(i, carry):
            for u in range(unroll):
                r = i * unroll + u
                tok = row_tok_ref[sub * tm + r]
                pltpu.make_async_copy(x_hbm.at[pl.ds(tok, 1), :], buf_ref.at[slot, pl.ds(r, 1), :],
                                      sems.at[slot]).start(priority=u % 2)
            return carry
        lax.fori_loop(0, tm // unroll, issue, 0)

    slot = lax.rem(s, 2)
    valid = sub_valid_ref[s] > 0

    @pl.when((s == 0) & valid)
    def _():
        start_sub(0, 0)

    nxt = jnp.minimum(s + 1, n_s - 1)

    @pl.when((s + 1 < n_s) & (sub_valid_ref[nxt] > 0))
    def _():
        start_sub(s + 1, 1 - slot)

    @pl.when(valid)
    def _():
        pltpu.make_async_copy(buf_ref.at[slot], buf_ref.at[slot], sems.at[slot]).wait()
        o_ref[...] = _rms_normalize(buf_ref[slot], g_ref[...]).astype(o_ref.dtype)

    @pl.when(jnp.logical_not(valid))
    def _():
        o_ref[...] = jnp.zeros(o_ref.shape, o_ref.dtype)


def dispatch(x, g, row_tok, sub_valid, n_rows):
    n, d = x.shape
    tm = MOE_SUB_TILE
    return pl.pallas_call(
        _dispatch_kernel,
        out_shape=jax.ShapeDtypeStruct((n_rows, d), BF16),
        grid_spec=pltpu.PrefetchScalarGridSpec(
            num_scalar_prefetch=2,
            grid=(n_rows // tm,),
            in_specs=[
                pl.BlockSpec(memory_space=pl.ANY),
                pl.BlockSpec((1, d), lambda s, rt, sv: (0, 0)),
            ],
            out_specs=pl.BlockSpec((tm, d), lambda s, rt, sv: (s, 0)),
            scratch_shapes=[pltpu.VMEM((2, tm, d), F32), pltpu.SemaphoreType.DMA((2,))],
        ),
        compiler_params=_params("arbitrary"),
        name="moe_dispatch",
    )(row_tok, sub_valid, x, g.reshape(1, d))


def _expert_kernel(tile_e_ref, tile_rows_ref, xs_ref, wg_ref, wu_ref, wd_ref, o_ref):
    t = pl.program_id(0)
    f = pl.program_id(1)
    n_sub = tile_rows_ref[t]

    @pl.when(f == 0)
    def _():
        o_ref[...] = jnp.zeros(o_ref.shape, o_ref.dtype)

    def swiglu(rows):
        xs = xs_ref[rows, :]
        gate = jnp.dot(xs, wg_ref[...].astype(BF16), preferred_element_type=F32)
        up = jnp.dot(xs, wu_ref[...].astype(BF16), preferred_element_type=F32)
        act = (_silu(gate) * up).astype(BF16)
        o_ref[rows, :] += jnp.dot(act, wd_ref[...].astype(BF16), preferred_element_type=F32)

    for occupied in range(1, MOE_TILE_SUBS + 1):
        @pl.when(n_sub == occupied)
        def _(occupied=occupied):
            swiglu(slice(0, occupied * MOE_SUB_TILE))


def expert_ffn(xs, wg, wu, wd, tile_e, tile_rows):
    n_rows, d = xs.shape
    fe = wg.shape[2]
    tm, tf = MOE_TILE_SUBS * MOE_SUB_TILE, MOE_F_TILE
    nf = fe // tf

    def f_idx(t, f, tr):
        return jnp.where(tr[t] > 0, f, nf - 1)

    return pl.pallas_call(
        _expert_kernel,
        out_shape=jax.ShapeDtypeStruct((n_rows, d), F32),
        grid_spec=pltpu.PrefetchScalarGridSpec(
            num_scalar_prefetch=2,
            grid=(n_rows // tm, nf),
            in_specs=[
                pl.BlockSpec((tm, d), lambda t, f, te, tr: (t, 0)),
                pl.BlockSpec((None, d, tf), lambda t, f, te, tr: (te[t], 0, f_idx(t, f, tr))),
                pl.BlockSpec((None, d, tf), lambda t, f, te, tr: (te[t], 0, f_idx(t, f, tr))),
                pl.BlockSpec((None, tf, d), lambda t, f, te, tr: (te[t], f_idx(t, f, tr), 0)),
            ],
            out_specs=pl.BlockSpec((tm, d), lambda t, f, te, tr: (t, 0), pipeline_mode=pl.Buffered(1)),
        ),
        compiler_params=_params("arbitrary", "arbitrary"),
        name="moe_experts",
    )(tile_e, tile_rows, xs, wg, wu, wd)


def _combine_kernel(dest_ref, x_ref, gates_ref, g_ref, ys_hbm, o_ref, buf_ref, sems):
    t = pl.program_id(0)
    n_t = pl.num_programs(0)
    tm = x_ref.shape[0]
    unroll = 4

    def start_tile(tile, slot):
        def issue(i, carry):
            for u in range(unroll):
                r = i * unroll + u
                for k in range(TOP_K):
                    row = dest_ref[(tile * tm + r) * TOP_K + k]
                    pltpu.make_async_copy(ys_hbm.at[pl.ds(row, 1), :], buf_ref.at[slot, k, pl.ds(r, 1), :],
                                          sems.at[slot]).start(priority=k)
            return carry
        lax.fori_loop(0, tm // unroll, issue, 0)

    slot = lax.rem(t, 2)

    @pl.when(t == 0)
    def _():
        start_tile(0, 0)

    @pl.when(t + 1 < n_t)
    def _():
        start_tile(t + 1, 1 - slot)

    pltpu.make_async_copy(buf_ref.at[slot], buf_ref.at[slot], sems.at[slot]).wait()
    gates = gates_ref[...]
    y = x_ref[...] + gates[:, 2:3] * buf_ref[slot, 0] + gates[:, 3:4] * buf_ref[slot, 1]
    o_ref[...] = _rms_normalize(y, g_ref[...])


def combine(x, gates, final_g, ys, dest):
    n, d = x.shape
    tm = COMBINE_TILE
    return pl.pallas_call(
        _combine_kernel,
        out_shape=jax.ShapeDtypeStruct((n, d), F32),
        grid_spec=pltpu.PrefetchScalarGridSpec(
            num_scalar_prefetch=1,
            grid=(n // tm,),
            in_specs=[
                pl.BlockSpec((tm, d), lambda t, dst: (t, 0)),
                pl.BlockSpec((tm, N_EXPERTS), lambda t, dst: (t, 0)),
                pl.BlockSpec((1, d), lambda t, dst: (0, 0)),
                pl.BlockSpec(memory_space=pl.ANY),
            ],
            out_specs=pl.BlockSpec((tm, d), lambda t, dst: (t, 0)),
            scratch_shapes=[pltpu.VMEM((2, TOP_K, tm, d), F32), pltpu.SemaphoreType.DMA((2,))],
        ),
        compiler_params=_params("arbitrary"),
        name="moe_combine",
    )(dest, x, gates, final_g.reshape(1, d), ys)


def _routing_tables(gates_out, n_tok):
    sub, slots = MOE_SUB_TILE, MOE_TILE_SUBS
    tm = slots * sub
    n_asg = n_tok * TOP_K
    max_subs = n_asg // sub + N_EXPERTS
    n_tiles = (max_subs + N_EXPERTS * (slots - 1)) // slots
    n_rows = n_tiles * tm
    i32 = jnp.int32
    flat_e = gates_out[:, :TOP_K].astype(i32).reshape(-1)
    onehot = (flat_e[:, None] == jnp.arange(N_EXPERTS, dtype=i32)[None, :]).astype(i32)
    csum = jnp.cumsum(onehot, axis=0)
    rank = jnp.sum(csum * onehot, axis=1) - 1
    counts = csum[-1]
    subs = (counts + sub - 1) // sub
    tiles = (subs + slots - 1) // slots
    base = subs // jnp.maximum(tiles, 1)
    extra = subs - base * tiles
    tile_end = jnp.cumsum(tiles)
    tile_start = tile_end - tiles

    pick = lambda per_expert: jnp.sum(per_expert[None, :] * onehot, axis=1)
    a_base, a_extra, a_start = pick(base), pick(extra), pick(tile_start)
    q = rank // sub
    big = a_extra * (a_base + 1)
    small_base = jnp.maximum(a_base, 1)
    k = jnp.where(q < big, q // (a_base + 1), a_extra + (q - big) // small_base)
    slot = jnp.where(q < big, q % (a_base + 1), (q - big) % small_base)
    dest = ((a_start + k) * tm + slot * sub + rank % sub).astype(i32)
    row_tok = jnp.zeros((n_rows,), i32).at[dest].set(jnp.arange(n_asg, dtype=i32) // TOP_K)

    n_used = tile_end[-1]
    tile_id = jnp.minimum(jnp.arange(n_tiles, dtype=i32), n_used - 1)
    tile_e = jnp.minimum(jnp.sum((tile_end[None, :] <= tile_id[:, None]).astype(i32), axis=1),
                         N_EXPERTS - 1).astype(i32)
    k_tile = tile_id - tile_start[tile_e]
    occupied = base[tile_e] + (k_tile < extra[tile_e]).astype(i32)
    tile_rows = jnp.where(jnp.arange(n_tiles, dtype=i32) < n_used, occupied, 0).astype(i32)
    sub_valid = (jnp.arange(slots, dtype=i32)[None, :] < tile_rows[:, None]).astype(i32)
    return dest, row_tok, tile_e, tile_rows, sub_valid.reshape(-1), n_rows


def kernel(x, attn_a_norm_g, w_qkv_a, w_o_a, kv_norm_g, w_kv, attn_b_norm_g, w_q_b,
           lam_q1, lam_k1, lam_q2, lam_k2, subln_g, w_o_b, ffn_norm_g, w_gate, w_up,
           w_down, w_router, w_exp_gate, w_exp_up, w_exp_down, final_norm_g):
    bsz, seq, d = x.shape
    n = bsz * seq
    n_heads_a = d // HEAD_DIM
    n_heads_b = d // (2 * HEAD_DIM)
    assert ffn_norm_g.shape[0] == 2 and attn_a_norm_g.shape[0] == 1 and attn_b_norm_g.shape[0] == 1
    bf = lambda w: w.astype(BF16)

    xf = x.reshape(n, d)

    qkv = norm_matmul(xf, attn_a_norm_g[0], w_qkv_a[0], HEAD_DIM)
    o = moba_attention(qkv, bsz, seq, n_heads_a)
    xf = grouped_matmul_residual(o, bf(w_o_a[0]), xf)
    act = norm_swiglu_up(xf, ffn_norm_g[0], w_gate[0], w_up[0])
    xf = matmul_residual(act, bf(w_down[0]), xf)

    kv = norm_matmul(xf, kv_norm_g, w_kv, 2 * HEAD_DIM)
    q = norm_matmul(xf, attn_b_norm_g[0], bf(w_q_b[0]), 2 * HEAD_DIM)
    lam_init = 0.8 - 0.6 * math.exp(-0.3 * 1)
    o = diff_attention(q, kv, lam_q1[0], lam_k1[0], lam_q2[0], lam_k2[0],
                       subln_g[0], lam_init, bsz, seq, n_heads_b)
    xf = grouped_matmul_residual(o, bf(w_o_b[0]), xf)
    gates_out = router(xf, ffn_norm_g[1], w_router[0])
    dest, row_tok, tile_e, tile_rows, sub_valid, n_rows = _routing_tables(gates_out, n)
    xs = dispatch(xf, ffn_norm_g[1], row_tok, sub_valid, n_rows)
    ys = expert_ffn(xs, w_exp_gate[0], w_exp_up[0], w_exp_down[0], tile_e, tile_rows)
    out = combine(xf, gates_out, final_norm_g, ys, dest)
    return out.reshape(bsz, seq, d)
```

```python
import functools
import math

import jax
import jax.numpy as jnp
from jax import lax
from jax.experimental import pallas as pl
from jax.experimental.pallas import tpu as pltpu

F32 = jnp.float32
BF16 = jnp.bfloat16

HEAD_DIM = 128
MOBA_BLOCK = 256
MOBA_TOPK = 3
N_EXPERTS = 8
TOP_K = 2
RMS_EPS = 1e-6
NEG_INF = -1e30

VMEM_LIMIT_BYTES = 56 * 1024 * 1024

ROW_TILE = 1024
COL_TILE = 1024
FFN_COL_TILE = 512
ATT_BLOCK = 256
MOBA_HEADS_PER_STEP = 4
DIFF_HEADS_PER_STEP = 2
MOE_SUB_TILE = 256
MOE_TILE_SUBS = 5
MOE_F_TILE = 512
ROUTER_TILE = 1024
COMBINE_TILE = 512

_NT = (((1,), (1,)), ((), ()))


def _params(*sem):
    return pltpu.CompilerParams(dimension_semantics=sem, vmem_limit_bytes=VMEM_LIMIT_BYTES)


def _rms_normalize(x, g):
    ms = jnp.mean(x * x, axis=-1, keepdims=True)
    return (x * lax.rsqrt(ms + RMS_EPS)) * g


def _norm_matmul_kernel(x_ref, g_ref, w_ref, o_ref, xn_ref):
    @pl.when(pl.program_id(1) == 0)
    def _():
        xn_ref[...] = _rms_normalize(x_ref[...], g_ref[...]).astype(BF16)

    w = w_ref[...].astype(BF16)
    res = jnp.dot(xn_ref[...], w, preferred_element_type=F32).astype(o_ref.dtype)
    n_groups, _, width = o_ref.shape
    for c in range(n_groups):
        o_ref[c] = res[:, c * width:(c + 1) * width]


def norm_matmul(x, g, w, width):
    n, d = x.shape
    cols = w.shape[1]
    tm, tn = ROW_TILE, min(COL_TILE, cols)
    return pl.pallas_call(
        _norm_matmul_kernel,
        out_shape=jax.ShapeDtypeStruct((cols // width, n, width), BF16),
        grid=(n // tm, cols // tn),
        in_specs=[
            pl.BlockSpec((tm, d), lambda i, j: (i, 0)),
            pl.BlockSpec((1, d), lambda i, j: (0, 0)),
            pl.BlockSpec((d, tn), lambda i, j: (0, j)),
        ],
        out_specs=pl.BlockSpec((tn // width, tm, width), lambda i, j: (j, i, 0)),
        scratch_shapes=[pltpu.VMEM((tm, d), BF16)],
        compiler_params=_params("parallel", "arbitrary"),
        name="norm_matmul",
    )(x, g.reshape(1, d), w)


def _silu(g):
    return g * (1.0 / (1.0 + jnp.exp(-g)))


def _norm_swiglu_up_kernel(x_ref, g_ref, wg_ref, wu_ref, o_ref, xn_ref):
    @pl.when(pl.program_id(1) == 0)
    def _():
        xn_ref[...] = _rms_normalize(x_ref[...], g_ref[...]).astype(BF16)

    xn = xn_ref[...]
    gate = jnp.dot(xn, wg_ref[...].astype(BF16), preferred_element_type=F32)
    up = jnp.dot(xn, wu_ref[...].astype(BF16), preferred_element_type=F32)
    o_ref[...] = (_silu(gate) * up).astype(o_ref.dtype)


def norm_swiglu_up(x, g, wg, wu):
    n, d = x.shape
    f = wg.shape[1]
    tm, tf = ROW_TILE, FFN_COL_TILE
    return pl.pallas_call(
        _norm_swiglu_up_kernel,
        out_shape=jax.ShapeDtypeStruct((n, f), BF16),
        grid=(n // tm, f // tf),
        in_specs=[
            pl.BlockSpec((tm, d), lambda i, j: (i, 0)),
            pl.BlockSpec((1, d), lambda i, j: (0, 0)),
            pl.BlockSpec((d, tf), lambda i, j: (0, j)),
            pl.BlockSpec((d, tf), lambda i, j: (0, j)),
        ],
        out_specs=pl.BlockSpec((tm, tf), lambda i, j: (i, j)),
        scratch_shapes=[pltpu.VMEM((tm, d), BF16)],
        compiler_params=_params("parallel", "arbitrary"),
        name="norm_swiglu_up",
    )(x, g.reshape(1, d), wg, wu)


def _matmul_residual_kernel(a_ref, w_ref, r_ref, o_ref):
    o_ref[...] = r_ref[...] + jnp.dot(a_ref[...], w_ref[...], preferred_element_type=F32)


def _grouped_matmul_residual_kernel(a_ref, w_ref, r_ref, o_ref):
    a = jnp.concatenate([a_ref[c] for c in range(a_ref.shape[0])], axis=1)
    o_ref[...] = r_ref[...] + jnp.dot(a, w_ref[...], preferred_element_type=F32)


def grouped_matmul_residual(a, w, res):
    groups, n, width = a.shape
    k = groups * width
    cols = w.shape[1]
    tm, tn = ROW_TILE, min(COL_TILE, cols)
    return pl.pallas_call(
        _grouped_matmul_residual_kernel,
        out_shape=jax.ShapeDtypeStruct((n, cols), F32),
        grid=(n // tm, cols // tn),
        in_specs=[
            pl.BlockSpec((groups, tm, width), lambda i, j: (0, i, 0)),
            pl.BlockSpec((k, tn), lambda i, j: (0, j)),
            pl.BlockSpec((tm, tn), lambda i, j: (i, j)),
        ],
        out_specs=pl.BlockSpec((tm, tn), lambda i, j: (i, j)),
        compiler_params=_params("parallel", "arbitrary"),
        name="grouped_matmul_residual",
    )(a, w, res)


def matmul_residual(a, w, res):
    n, k = a.shape
    cols = w.shape[1]
    tm, tn = ROW_TILE, min(COL_TILE, cols)
    while 2 * (2 * k * (tm + tn) + 8 * tm * tn) + 4 * tm * tn > VMEM_LIMIT_BYTES:
        tn //= 2
    return pl.pallas_call(
        _matmul_residual_kernel,
        out_shape=jax.ShapeDtypeStruct((n, cols), F32),
        grid=(n // tm, cols // tn),
        in_specs=[
            pl.BlockSpec((tm, k), lambda i, j: (i, 0)),
            pl.BlockSpec((k, tn), lambda i, j: (0, j)),
            pl.BlockSpec((tm, tn), lambda i, j: (i, j)),
        ],
        out_specs=pl.BlockSpec((tm, tn), lambda i, j: (i, j)),
        compiler_params=_params("parallel", "arbitrary"),
        name="matmul_residual",
    )(a, w, res)


_N_GAMMA_PARTS = 3
_OFF_LANE = 8
_BLK_LANE = _OFF_LANE + _N_GAMMA_PARTS


def _key_aug_table(seq):
    pos = jnp.arange(seq, dtype=jnp.int32)[:, None]
    lane = jnp.arange(HEAD_DIM, dtype=jnp.int32)[None, :]
    off = (pos % ATT_BLOCK).astype(F32)
    base = (pos - pos % ATT_BLOCK).astype(F32)
    tab = jnp.where(lane == pos // ATT_BLOCK, 1.0, 0.0)
    tab = jnp.where((lane >= _OFF_LANE) & (lane < _BLK_LANE), off, tab)
    tab = jnp.where((lane >= _BLK_LANE) & (lane < _BLK_LANE + _N_GAMMA_PARTS), base, tab)
    return tab.astype(BF16)


def _query_aug_rows(n_heads):
    hh = jnp.arange(1, n_heads + 1, dtype=F32)
    gamma = jnp.exp2(-8.0 * hh / n_heads) * (HEAD_DIM ** 0.5)
    parts = []
    rest = gamma
    for _ in range(_N_GAMMA_PARTS):
        part = rest.astype(BF16).astype(F32)
        parts.append(part)
        rest = rest - part
    lane = jnp.arange(HEAD_DIM, dtype=jnp.int32)[None, :]
    rows = jnp.zeros((n_heads, HEAD_DIM), F32)
    for p, part in enumerate(parts):
        rows = jnp.where((lane == _OFF_LANE + p) | (lane == _BLK_LANE + p), part[:, None], rows)
    return rows.reshape(n_heads, 1, HEAD_DIM)


_EXP2_SCALE = (HEAD_DIM ** -0.5) * math.log2(math.e)


def _causal_mask():
    blk = ATT_BLOCK
    return (lax.broadcasted_iota(jnp.int32, (blk, blk), 0)
            >= lax.broadcasted_iota(jnp.int32, (blk, blk), 1))


def _block_softmax(q_aug, kaug_ref, i, tri):
    blk = ATT_BLOCK
    z_own = lax.dot_general(q_aug, kaug_ref[i * blk:(i + 1) * blk, :], _NT, preferred_element_type=F32)
    z_own = jnp.where(tri, z_own, NEG_INF)
    m = jnp.max(z_own, axis=1, keepdims=True)
    if i == 0:
        p_own = jnp.exp2((z_own - m) * _EXP2_SCALE)
        return None, p_own, jnp.sum(p_own, axis=1, keepdims=True)
    z_past = lax.dot_general(q_aug, kaug_ref[:i * blk, :], _NT, preferred_element_type=F32)
    m = jnp.maximum(m, jnp.max(z_past, axis=1, keepdims=True))
    p_own = jnp.exp2((z_own - m) * _EXP2_SCALE)
    p_past = jnp.exp2((z_past - m) * _EXP2_SCALE)
    l = jnp.sum(p_own, axis=1, keepdims=True) + jnp.sum(p_past, axis=1, keepdims=True)
    return p_past, p_own, l


def _moba_kernel(q_ref, k_ref, v_ref, ktab_ref, qrow_ref, o_ref, kaug_ref, kmean_ref, *, n_blk):
    for g in range(q_ref.shape[0]):
        _moba_head(q_ref.at[g], k_ref.at[g], v_ref.at[g], ktab_ref, qrow_ref.at[g], o_ref.at[g],
                   kaug_ref.at[g], kmean_ref.at[g], n_blk)


def _moba_head(q_ref, k_ref, v_ref, ktab_ref, qrow_ref, o_ref, kaug_ref, kmean_ref, n_blk):
    blk, hd = ATT_BLOCK, HEAD_DIM
    kaug_ref[:, :hd] = k_ref[...]
    kaug_ref[:, hd:] = ktab_ref[...]
    for j in range(n_blk):
        kj = k_ref[j * blk:(j + 1) * blk, :].astype(F32)
        kmean_ref[j:j + 1, :] = jnp.mean(kj, axis=0, keepdims=True)

    km = kmean_ref[...]
    km_hi = km.astype(BF16)
    km_lo = (km - km_hi.astype(F32)).astype(BF16)
    q_all = q_ref[...]
    gate_t = (lax.dot_general(km_hi, q_all, _NT, preferred_element_type=F32)
              + lax.dot_general(km_lo, q_all, _NT, preferred_element_type=F32))

    tri = _causal_mask()
    eye = (lax.broadcasted_iota(jnp.int32, (blk, blk), 0)
           == lax.broadcasted_iota(jnp.int32, (blk, blk), 1)).astype(BF16)
    blk_id = lax.broadcasted_iota(jnp.int32, (n_blk, blk), 0)
    q_row = qrow_ref[...]

    for i in range(n_blk):
        aug = jnp.broadcast_to(q_row, (blk, hd))
        if i > MOBA_TOPK:
            g = gate_t[:, i * blk:(i + 1) * blk]
            cnt = jnp.zeros(g.shape, F32)
            for jp in range(i):
                row = g[jp:jp + 1, :]
                beats = (row > g) | ((row == g) & (jp < blk_id))
                cnt = cnt + jnp.where(beats, 1.0, 0.0)
            drop_t = jnp.where((cnt >= float(MOBA_TOPK)) & (blk_id < i), 1.0, 0.0)
            drop_t = jnp.concatenate([drop_t, jnp.zeros((hd - n_blk, blk), F32)], axis=0)
            drop = lax.dot_general(eye, drop_t.astype(BF16), _NT, preferred_element_type=F32)
            aug = aug + drop * NEG_INF
        q_aug = jnp.concatenate([q_ref[i * blk:(i + 1) * blk, :], aug.astype(BF16)], axis=1)
        p_past, p_own, l = _block_softmax(q_aug, kaug_ref, i, tri)
        o = jnp.dot(p_own.astype(BF16), v_ref[i * blk:(i + 1) * blk, :], preferred_element_type=F32)
        if i > 0:
            o = o + jnp.dot(p_past.astype(BF16), v_ref[:i * blk, :], preferred_element_type=F32)
        o_ref[i * blk:(i + 1) * blk, :] = (o / l).astype(o_ref.dtype)


def moba_attention(qkv, bsz, seq, n_heads):
    n = bsz * seq
    blk, hd = ATT_BLOCK, HEAD_DIM
    n_blk = seq // blk
    hps = MOBA_HEADS_PER_STEP
    n_groups = n_heads // hps
    return pl.pallas_call(
        functools.partial(_moba_kernel, n_blk=n_blk),
        out_shape=jax.ShapeDtypeStruct((n_heads, n, hd), BF16),
        grid=(bsz, n_groups),
        in_specs=[
            pl.BlockSpec((hps, seq, hd), lambda b, h: (h, b, 0)),
            pl.BlockSpec((hps, seq, hd), lambda b, h: (n_groups + h, b, 0)),
            pl.BlockSpec((hps, seq, hd), lambda b, h: (2 * n_groups + h, b, 0)),
            pl.BlockSpec((seq, hd), lambda b, h: (0, 0)),
            pl.BlockSpec((hps, 1, hd), lambda b, h: (h, 0, 0)),
        ],
        out_specs=pl.BlockSpec((hps, seq, hd), lambda b, h: (h, b, 0)),
        scratch_shapes=[
            pltpu.VMEM((hps, seq, 2 * hd), BF16),
            pltpu.VMEM((hps, n_blk, hd), F32),
        ],
        compiler_params=_params("parallel", "parallel"),
        name="moba_attention",
    )(qkv, qkv, qkv, _key_aug_table(seq), _query_aug_rows(n_heads))


def _diff_kernel(q_ref, k_ref, v_ref, ktab_ref, qrow_ref, lq1_ref, lk1_ref, lq2_ref, lk2_ref, sg_ref,
                 o_ref, qaug_ref, kaug_ref, *, n_blk, lam_init):
    for g in range(q_ref.shape[0]):
        _diff_head(q_ref.at[g], k_ref.at[g], v_ref.at[g], ktab_ref, qrow_ref.at[g], lq1_ref, lk1_ref,
                   lq2_ref, lk2_ref, sg_ref, o_ref.at[g], qaug_ref.at[g], kaug_ref.at[g], n_blk, lam_init)


def _diff_head(q_ref, k_ref, v_ref, ktab_ref, qrow_ref, lq1_ref, lk1_ref, lq2_ref, lk2_ref, sg_ref,
               o_ref, qaug_ref, kaug_ref, n_blk, lam_init):
    blk, hd = ATT_BLOCK, HEAD_DIM
    seq = n_blk * blk
    q_row = jnp.broadcast_to(qrow_ref[...], (seq, hd)).astype(BF16)
    for c in range(2):
        qaug_ref[c, :, :hd] = q_ref[:, c * hd:(c + 1) * hd]
        qaug_ref[c, :, hd:] = q_row
        kaug_ref[c, :, :hd] = k_ref[:, c * hd:(c + 1) * hd]
        kaug_ref[c, :, hd:] = ktab_ref[...]

    lam = (jnp.exp(jnp.sum(lq1_ref[...] * lk1_ref[...], axis=1, keepdims=True))
           - jnp.exp(jnp.sum(lq2_ref[...] * lk2_ref[...], axis=1, keepdims=True))
           + lam_init)
    tri = _causal_mask()

    for i in range(n_blk):
        rows = slice(i * blk, (i + 1) * blk)
        p1_past, p1_own, l1 = _block_softmax(qaug_ref[0, rows, :], kaug_ref.at[0], i, tri)
        p2_past, p2_own, l2 = _block_softmax(qaug_ref[1, rows, :], kaug_ref.at[1], i, tri)
        r1 = 1.0 / l1
        r2 = lam / l2
        w_own = (p1_own * r1 - p2_own * r2).astype(BF16)
        o = jnp.dot(w_own, v_ref[rows, :], preferred_element_type=F32)
        if i > 0:
            w_past = (p1_past * r1 - p2_past * r2).astype(BF16)
            o = o + jnp.dot(w_past, v_ref[:i * blk, :], preferred_element_type=F32)
        o = _rms_normalize(o, sg_ref[...]) * (1.0 - lam_init)
        o_ref[rows, :] = o.astype(o_ref.dtype)


def diff_attention(q, kv, lq1, lk1, lq2, lk2, subln_g, lam_init, bsz, seq, n_heads):
    n = bsz * seq
    blk, hd = ATT_BLOCK, HEAD_DIM
    n_blk = seq // blk
    hps = DIFF_HEADS_PER_STEP
    n_groups = n_heads // hps
    vec = lambda a: a.reshape(1, hd)
    vec_spec = pl.BlockSpec((1, hd), lambda b, h: (0, 0))
    return pl.pallas_call(
        functools.partial(_diff_kernel, n_blk=n_blk, lam_init=lam_init),
        out_shape=jax.ShapeDtypeStruct((n_heads, n, 2 * hd), BF16),
        grid=(bsz, n_groups),
        in_specs=[
            pl.BlockSpec((hps, seq, 2 * hd), lambda b, h: (h, b, 0)),
            pl.BlockSpec((hps, seq, 2 * hd), lambda b, h: (h, b, 0)),
            pl.BlockSpec((hps, seq, 2 * hd), lambda b, h: (n_groups + h, b, 0)),
            pl.BlockSpec((seq, hd), lambda b, h: (0, 0)),
            pl.BlockSpec((hps, 1, hd), lambda b, h: (h, 0, 0)),
            vec_spec, vec_spec, vec_spec, vec_spec,
            pl.BlockSpec((1, 2 * hd), lambda b, h: (0, 0)),
        ],
        out_specs=pl.BlockSpec((hps, seq, 2 * hd), lambda b, h: (h, b, 0)),
        scratch_shapes=[
            pltpu.VMEM((hps, 2, seq, 2 * hd), BF16),
            pltpu.VMEM((hps, 2, seq, 2 * hd), BF16),
        ],
        compiler_params=_params("parallel", "parallel"),
        name="diff_attention",
    )(q, kv, kv, _key_aug_table(seq), _query_aug_rows(n_heads),
      vec(lq1), vec(lk1), vec(lq2), vec(lk2), subln_g.reshape(1, 2 * hd))


def _router_kernel(x_ref, g_ref, w_ref, o_ref):
    hn = _rms_normalize(x_ref[...], g_ref[...])
    w = w_ref[...]
    h_hi = hn.astype(BF16)
    h_lo = (hn - h_hi.astype(F32)).astype(BF16)
    w_hi = w.astype(BF16)
    w_lo = (w - w_hi.astype(F32)).astype(BF16)
    logits = (jnp.dot(h_hi, w_hi, preferred_element_type=F32)
              + jnp.dot(h_hi, w_lo, preferred_element_type=F32)
              + jnp.dot(h_lo, w_hi, preferred_element_type=F32))
    lane = lax.broadcasted_iota(jnp.int32, logits.shape, 1).astype(F32)
    big = float(N_EXPERTS)
    m1 = jnp.max(logits, axis=1, keepdims=True)
    e1 = jnp.min(jnp.where(logits == m1, lane, big), axis=1, keepdims=True)
    rest = jnp.where(lane == e1, -jnp.inf, logits)
    m2 = jnp.max(rest, axis=1, keepdims=True)
    e2 = jnp.min(jnp.where(rest == m2, lane, big), axis=1, keepdims=True)
    t = jnp.exp(m2 - m1)
    g1 = 1.0 / (1.0 + t)
    g2 = t / (1.0 + t)
    o_ref[...] = jnp.where(lane == 0.0, e1,
                 jnp.where(lane == 1.0, e2,
                 jnp.where(lane == 2.0, g1,
                 jnp.where(lane == 3.0, g2, 0.0))))


def router(x, g, w_router):
    n, d = x.shape
    tm = ROUTER_TILE
    return pl.pallas_call(
        _router_kernel,
        out_shape=jax.ShapeDtypeStruct((n, N_EXPERTS), F32),
        grid=(n // tm,),
        in_specs=[
            pl.BlockSpec((tm, d), lambda i: (i, 0)),
            pl.BlockSpec((1, d), lambda i: (0, 0)),
            pl.BlockSpec((d, N_EXPERTS), lambda i: (0, 0)),
        ],
        out_specs=pl.BlockSpec((tm, N_EXPERTS), lambda i: (i, 0)),
        compiler_params=_params("parallel"),
        name="router",
    )(x, g.reshape(1, d), w_router)


def _dispatch_kernel(row_tok_ref, sub_valid_ref, x_hbm, g_ref, o_ref, buf_ref, sems):
    s = pl.program_id(0)
    n_s = pl.num_programs(0)
    tm = o_ref.shape[0]
    unroll = 8

    def start_sub(sub, slot):
        def issue(i, carry):
            for u in range(unroll):
                r = i * unroll + u
                tok = row_tok_ref[sub * tm + r]
                pltpu.make_async_copy(x_hbm.at[pl.ds(tok, 1), :], buf_ref.at[slot, pl.ds(r, 1), :],
                                      sems.at[slot]).start(priority=u % 2)
            return carry
        lax.fori_loop(0, tm // unroll, issue, 0)

    slot = lax.rem(s, 2)
    valid = sub_valid_ref[s] > 0

    @pl.when((s == 0) & valid)
    def _():
        start_sub(0, 0)

    nxt = jnp.minimum(s + 1, n_s - 1)

    @pl.when((s + 1 < n_s) & (sub_valid_ref[nxt] > 0))
    def _():
        start_sub(s + 1, 1 - slot)

    @pl.when(valid)
    def _():
        pltpu.make_async_copy(buf_ref.at[slot], buf_ref.at[slot], sems.at[slot]).wait()
        o_ref[...] = _rms_normalize(buf_ref[slot], g_ref[...]).astype(o_ref.dtype)

    @pl.when(jnp.logical_not(valid))
    def _():
        o_ref[...] = jnp.zeros(o_ref.shape, o_ref.dtype)


def dispatch(x, g, row_tok, sub_valid, n_rows):
    n, d = x.shape
    tm = MOE_SUB_TILE
    return pl.pallas_call(
        _dispatch_kernel,
        out_shape=jax.ShapeDtypeStruct((n_rows, d), BF16),
        grid_spec=pltpu.PrefetchScalarGridSpec(
            num_scalar_prefetch=2,
            grid=(n_rows // tm,),
            in_specs=[
                pl.BlockSpec(memory_space=pl.ANY),
                pl.BlockSpec((1, d), lambda s, rt, sv: (0, 0)),
            ],
            out_specs=pl.BlockSpec((tm, d), lambda s, rt, sv: (s, 0)),
            scratch_shapes=[pltpu.VMEM((2, tm, d), F32), pltpu.SemaphoreType.DMA((2,))],
        ),
        compiler_params=_params("arbitrary"),
        name="moe_dispatch",
    )(row_tok, sub_valid, x, g.reshape(1, d))


def _expert_kernel(tile_e_ref, tile_rows_ref, xs_ref, wg_ref, wu_ref, wd_ref, o_ref):
    t = pl.program_id(0)
    f = pl.program_id(1)
    n_sub = tile_rows_ref[t]

    @pl.when(f == 0)
    def _():
        o_ref[...] = jnp.zeros(o_ref.shape, o_ref.dtype)

    def swiglu(rows):
        xs = xs_ref[rows, :]
        gate = jnp.dot(xs, wg_ref[...].astype(BF16), preferred_element_type=F32)
        up = jnp.dot(xs, wu_ref[...].astype(BF16), preferred_element_type=F32)
        act = (_silu(gate) * up).astype(BF16)
        o_ref[rows, :] += jnp.dot(act, wd_ref[...].astype(BF16), preferred_element_type=F32)

    for occupied in range(1, MOE_TILE_SUBS + 1):
        @pl.when(n_sub == occupied)
        def _(occupied=occupied):
            swiglu(slice(0, occupied * MOE_SUB_TILE))


def expert_ffn(xs, wg, wu, wd, tile_e, tile_rows):
    n_rows, d = xs.shape
    fe = wg.shape[2]
    tm, tf = MOE_TILE_SUBS * MOE_SUB_TILE, MOE_F_TILE
    nf = fe // tf

    def f_idx(t, f, tr):
        return jnp.where(tr[t] > 0, f, nf - 1)

    return pl.pallas_call(
        _expert_kernel,
        out_shape=jax.ShapeDtypeStruct((n_rows, d), F32),
        grid_spec=pltpu.PrefetchScalarGridSpec(
            num_scalar_prefetch=2,
            grid=(n_rows // tm, nf),
            in_specs=[
                pl.BlockSpec((tm, d), lambda t, f, te, tr: (t, 0)),
                pl.BlockSpec((None, d, tf), lambda t, f, te, tr: (te[t], 0, f_idx(t, f, tr))),
                pl.BlockSpec((None, d, tf), lambda t, f, te, tr: (te[t], 0, f_idx(t, f, tr))),
                pl.BlockSpec((None, tf, d), lambda t, f, te, tr: (te[t], f_idx(t, f, tr), 0)),
            ],
            out_specs=pl.BlockSpec((tm, d), lambda t, f, te, tr: (t, 0), pipeline_mode=pl.Buffered(1)),
        ),
        compiler_params=_params("arbitrary", "arbitrary"),
        name="moe_experts",
    )(tile_e, tile_rows, xs, wg, wu, wd)


def _combine_kernel(dest_ref, x_ref, gates_ref, g_ref, ys_hbm, o_ref, buf_ref, sems):
    t = pl.program_id(0)
    n_t = pl.num_programs(0)
    tm = x_ref.shape[0]
    unroll = 4

    def start_tile(tile, slot):
        def issue(i, carry):
            for u in range(unroll):
                r = i * unroll + u
                for k in range(TOP_K):
                    row = dest_ref[(tile * tm + r) * TOP_K + k]
                    pltpu.make_async_copy(ys_hbm.at[pl.ds(row, 1), :], buf_ref.at[slot, k, pl.ds(r, 1), :],
                                          sems.at[slot]).start(priority=k)
            return carry
        lax.fori_loop(0, tm // unroll, issue, 0)

    slot = lax.rem(t, 2)

    @pl.when(t == 0)
    def _():
        start_tile(0, 0)

    @pl.when(t + 1 < n_t)
    def _():
        start_tile(t + 1, 1 - slot)

    pltpu.make_async_copy(buf_ref.at[slot], buf_ref.at[slot], sems.at[slot]).wait()
    gates = gates_ref[...]
    y = x_ref[...] + gates[:, 2:3] * buf_ref[slot, 0] + gates[:, 3:4] * buf_ref[slot, 1]
    o_ref[...] = _rms_normalize(y, g_ref[...])


def combine(x, gates, final_g, ys, dest):
    n, d = x.shape
    tm = COMBINE_TILE
    return pl.pallas_call(
        _combine_kernel,
        out_shape=jax.ShapeDtypeStruct((n, d), F32),
        grid_spec=pltpu.PrefetchScalarGridSpec(
            num_scalar_prefetch=1,
            grid=(n // tm,),
            in_specs=[
                pl.BlockSpec((tm, d), lambda t, dst: (t, 0)),
                pl.BlockSpec((tm, N_EXPERTS), lambda t, dst: (t, 0)),
                pl.BlockSpec((1, d), lambda t, dst: (0, 0)),
                pl.BlockSpec(memory_space=pl.ANY),
            ],
            out_specs=pl.BlockSpec((tm, d), lambda t, dst: (t, 0)),
            scratch_shapes=[pltpu.VMEM((2, TOP_K, tm, d), F32), pltpu.SemaphoreType.DMA((2,))],
        ),
        compiler_params=_params("arbitrary"),
        name="moe_combine",
    )(dest, x, gates, final_g.reshape(1, d), ys)


def _routing_tables(gates_out, n_tok):
    sub, slots = MOE_SUB_TILE, MOE_TILE_SUBS
    tm = slots * sub
    n_asg = n_tok * TOP_K
    max_subs = n_asg // sub + N_EXPERTS
    n_tiles = (max_subs + N_EXPERTS * (slots - 1)) // slots
    n_rows = n_tiles * tm
    i32 = jnp.int32
    flat_e = gates_out[:, :TOP_K].astype(i32).reshape(-1)
    onehot = (flat_e[:, None] == jnp.arange(N_EXPERTS, dtype=i32)[None, :]).astype(i32)
    csum = jnp.cumsum(onehot, axis=0)
    rank = jnp.sum(csum * onehot, axis=1) - 1
    counts = csum[-1]
    subs = (counts + sub - 1) // sub
    tiles = (subs + slots - 1) // slots
    base = subs // jnp.maximum(tiles, 1)
    extra = subs - base * tiles
    tile_end = jnp.cumsum(tiles)
    tile_start = tile_end - tiles

    pick = lambda per_expert: jnp.sum(per_expert[None, :] * onehot, axis=1)
    a_base, a_extra, a_start = pick(base), pick(extra), pick(tile_start)
    q = rank // sub
    big = a_extra * (a_base + 1)
    small_base = jnp.maximum(a_base, 1)
    k = jnp.where(q < big, q // (a_base + 1), a_extra + (q - big) // small_base)
    slot = jnp.where(q < big, q % (a_base + 1), (q - big) % small_base)
    dest = ((a_start + k) * tm + slot * sub + rank % sub).astype(i32)
    row_tok = jnp.zeros((n_rows,), i32).at[dest].set(
        jnp.arange(n_asg, dtype=i32) // TOP_K, unique_indices=True, mode="promise_in_bounds")

    n_used = tile_end[-1]
    tile_id = jnp.minimum(jnp.arange(n_tiles, dtype=i32), n_used - 1)
    tile_e = jnp.minimum(jnp.sum((tile_end[None, :] <= tile_id[:, None]).astype(i32), axis=1),
                         N_EXPERTS - 1).astype(i32)
    k_tile = tile_id - tile_start[tile_e]
    occupied = base[tile_e] + (k_tile < extra[tile_e]).astype(i32)
    tile_rows = jnp.where(jnp.arange(n_tiles, dtype=i32) < n_used, occupied, 0).astype(i32)
    sub_valid = (jnp.arange(slots, dtype=i32)[None, :] < tile_rows[:, None]).astype(i32)
    return dest, row_tok, tile_e, tile_rows, sub_valid.reshape(-1), n_rows


def kernel(x, attn_a_norm_g, w_qkv_a, w_o_a, kv_norm_g, w_kv, attn_b_norm_g, w_q_b,
           lam_q1, lam_k1, lam_q2, lam_k2, subln_g, w_o_b, ffn_norm_g, w_gate, w_up,
           w_down, w_router, w_exp_gate, w_exp_up, w_exp_down, final_norm_g):
    bsz, seq, d = x.shape
    n = bsz * seq
    n_heads_a = d // HEAD_DIM
    n_heads_b = d // (2 * HEAD_DIM)
    assert ffn_norm_g.shape[0] == 2 and attn_a_norm_g.shape[0] == 1 and attn_b_norm_g.shape[0] == 1
    bf = lambda w: w.astype(BF16)

    xf = x.reshape(n, d)

    qkv = norm_matmul(xf, attn_a_norm_g[0], w_qkv_a[0], HEAD_DIM)
    o = moba_attention(qkv, bsz, seq, n_heads_a)
    xf = grouped_matmul_residual(o, bf(w_o_a[0]), xf)
    act = norm_swiglu_up(xf, ffn_norm_g[0], w_gate[0], w_up[0])
    xf = matmul_residual(act, bf(w_down[0]), xf)

    kv = norm_matmul(xf, kv_norm_g, w_kv, 2 * HEAD_DIM)
    q = norm_matmul(xf, attn_b_norm_g[0], bf(w_q_b[0]), 2 * HEAD_DIM)
    lam_init = 0.8 - 0.6 * math.exp(-0.3 * 1)
    o = diff_attention(q, kv, lam_q1[0], lam_k1[0], lam_q2[0], lam_k2[0],
                       subln_g[0], lam_init, bsz, seq, n_heads_b)
    xf = grouped_matmul_residual(o, bf(w_o_b[0]), xf)
    gates_out = router(xf, ffn_norm_g[1], w_router[0])
    dest, row_tok, tile_e, tile_rows, sub_valid, n_rows = _routing_tables(gates_out, n)
    xs = dispatch(xf, ffn_norm_g[1], row_tok, sub_valid, n_rows)
    ys = expert_ffn(xs, w_exp_gate[0], w_exp_up[0], w_exp_down[0], tile_e, tile_rows)
    out = combine(xf, gates_out, final_norm_g, ys, dest)
    return out.reshape(bsz, seq, d)
```
